```python
import math
import jax, jax.numpy as jnp
from jax import lax
import numpy as np

D_MODEL = 1024
BATCH = 4
SEQ = 8192
DEPTH = 4

D_FF = 2816
NORM_EPS = 1e-6
CHUNK = 64
LOG_FLOOR = 1e-20

A_HEADS = 4
A_DK = 128
A_DV = 64
B_HEADS = 4
B_DK = 128
B_DV = 128
CONV_K = 4
C_HEADS = 4
C_DK = 64
C_DV = 64
ROPE_BASE = 10000.0

MIX_WIDTH = A_HEADS * A_DV + B_HEADS * B_DV + C_HEADS * C_DV
IN_SPLITS = (
    A_HEADS * A_DK, A_HEADS * A_DK, A_HEADS * A_DV, A_HEADS * A_DV,
    B_HEADS * B_DK, B_HEADS * B_DK, B_HEADS * B_DV, B_HEADS * B_DV,
    B_HEADS, B_HEADS,
    C_HEADS * C_DK, C_HEADS * C_DK, C_HEADS * C_DV, C_HEADS * C_DV,
)
D_IN = int(sum(IN_SPLITS))
SPLIT_IDX = tuple(int(v) for v in np.cumsum(IN_SPLITS)[:-1])
GDN_CONV_CH = B_HEADS * (2 * B_DK + B_DV)

kernel_name = "hybrid_hgrn2_gdn_retention_macaron"


def rms_norm(x, w):
    xf = x.astype(jnp.float32)
    y = xf * lax.rsqrt(jnp.mean(xf * xf, axis=-1, keepdims=True) + NORM_EPS)
    return (y * w.astype(jnp.float32)).astype(x.dtype)


def head_rms(o):
    return o * lax.rsqrt(jnp.mean(o * o, axis=-1, keepdims=True) + NORM_EPS)


def masked_exp(d, mask):
    return jnp.where(mask, jnp.exp(jnp.where(mask, d, 0.0)), 0.0)


def swiglu_ffn(h, wg, wu, wd):
    return (jax.nn.silu(h @ wg) * (h @ wu)) @ wd


def split_heads(t, n_heads):
    return t.reshape(t.shape[:-1] + (n_heads, t.shape[-1] // n_heads))


def to_chunks(t):
    b, s, h, d = t.shape
    return t.reshape(b, s // CHUNK, CHUNK, h, d).transpose(0, 3, 1, 2, 4)


def from_chunks(t):
    b, h, n, c, d = t.shape
    return t.transpose(0, 2, 3, 1, 4).reshape(b, n * c, h, d)


def hgrn2_lower_bounds(p):
    s = jax.nn.softmax(p.astype(jnp.float32), axis=0)
    return jnp.cumsum(s, axis=0) - s[0:1]


def hgrn2_mixer(q, f_logit, i_in, gate, lb, norm_w):
    f32 = jnp.float32
    q = jax.nn.silu(split_heads(q, A_HEADS).astype(f32))
    z = split_heads(f_logit, A_HEADS).astype(f32)
    v = split_heads(i_in, A_HEADS).astype(f32)
    lb = lb.reshape(A_HEADS, A_DK)
    f = lb + (1.0 - lb) * jax.nn.sigmoid(z)
    log_f = jnp.log(jnp.maximum(f, LOG_FLOOR))
    k = (1.0 - lb) * jax.nn.sigmoid(-z)
    qc, kc, vc, lfc = to_chunks(q), to_chunks(k), to_chunks(v), to_chunks(log_f)
    b = jnp.cumsum(lfc, axis=3)
    b_last = b[:, :, :, -1:, :]
    q_in = qc * jnp.exp(b)
    k_st = kc * jnp.exp(b_last - b)
    decay_st = jnp.exp(b_last[:, :, :, 0, :])
    causal = jnp.tril(jnp.ones((CHUNK, CHUNK), dtype=bool))[:, :, None]

    def step(S, xs):
        q_c, k_c, v_c, b_c, qi, ks, dl = xs
        diff = b_c[:, :, :, None, :] - b_c[:, :, None, :, :]
        dec = masked_exp(diff, causal)
        attn = jnp.einsum('bhik,bhijk,bhjk->bhij', q_c, dec, k_c)
        o = jnp.einsum('bhik,bhkv->bhiv', qi, S) + jnp.einsum('bhij,bhjv->bhiv', attn, v_c)
        S = dl[..., None] * S + jnp.einsum('bhjk,bhjv->bhkv', ks, v_c)
        return S, o

    xs = tuple(jnp.moveaxis(t, 2, 0) for t in (qc, kc, vc, b, q_in, k_st, decay_st))
    S0 = jnp.zeros((q.shape[0], A_HEADS, A_DK, A_DV), f32)
    _, o = lax.scan(step, S0, xs)
    o = from_chunks(jnp.moveaxis(o, 0, 2))
    g = split_heads(gate, A_HEADS).astype(f32)
    o = head_rms(o) * norm_w.astype(f32) * jax.nn.silu(g)
    return o.reshape(o.shape[0], o.shape[1], A_HEADS * A_DV)


def causal_depthwise_conv(x, w):
    ch = x.shape[-1]
    return lax.conv_general_dilated(
        x, w[:, None, :].astype(x.dtype), window_strides=(1,), padding=[(CONV_K - 1, 0)],
        dimension_numbers=('NWC', 'WIO', 'NWC'), feature_group_count=ch)


def gated_deltanet_mixer(q, k, v, z, beta_logit, a_logit, conv_w, a_log, dt_bias, norm_w):
    f32 = jnp.float32
    qkv = jax.nn.silu(causal_depthwise_conv(jnp.concatenate([q, k, v], axis=-1), conv_w)).astype(f32)
    q, k, v = jnp.split(qkv, [B_HEADS * B_DK, 2 * B_HEADS * B_DK], axis=-1)
    q, k, v = split_heads(q, B_HEADS), split_heads(k, B_HEADS), split_heads(v, B_HEADS)
    q = q * lax.rsqrt(jnp.sum(q * q, -1, keepdims=True) + NORM_EPS) * (B_DK ** -0.5)
    k = k * lax.rsqrt(jnp.sum(k * k, -1, keepdims=True) + NORM_EPS)
    beta = jax.nn.sigmoid(beta_logit.astype(f32))
    g = -jnp.exp(a_log.astype(f32)) * jax.nn.softplus(a_logit.astype(f32) + dt_bias.astype(f32))
    qc, kc, vc = to_chunks(q), to_chunks(k), to_chunks(v)
    bc = to_chunks(beta[..., None])[..., 0]
    gc = jnp.cumsum(to_chunks(g[..., None])[..., 0], axis=-1)
    idx = jnp.arange(CHUNK)
    incl = idx[:, None] >= idx[None, :]
    strict = idx[:, None] > idx[None, :]
    diff = gc[..., :, None] - gc[..., None, :]
    m = bc[..., :, None] * jnp.einsum('bhnik,bhnjk->bhnij', kc, kc) * masked_exp(diff, strict)
    a_mat = m + jnp.eye(CHUNK, dtype=f32)
    rhs = jnp.concatenate([vc * bc[..., None], kc * (bc * jnp.exp(gc))[..., None]], axis=-1)
    sol = lax.linalg.triangular_solve(a_mat, rhs, left_side=True, lower=True, unit_diagonal=True)
    u, w = sol[..., :B_DV], sol[..., B_DV:]
    attn = jnp.einsum('bhnik,bhnjk->bhnij', qc, kc) * masked_exp(diff, incl)
    q_in = qc * jnp.exp(gc)[..., None]
    k_st = kc * jnp.exp(gc[..., -1:] - gc)[..., None]
    decay_st = jnp.exp(gc[..., -1])

    def step(S, xs):
        u_c, w_c, attn_c, qi, ks, dl = xs
        v_new = u_c - jnp.einsum('bhck,bhkv->bhcv', w_c, S)
        o = jnp.einsum('bhik,bhkv->bhiv', qi, S) + jnp.einsum('bhij,bhjv->bhiv', attn_c, v_new)
        S = dl[..., None, None] * S + jnp.einsum('bhjk,bhjv->bhkv', ks, v_new)
        return S, o

    xs = tuple(jnp.moveaxis(t, 2, 0) for t in (u, w, attn, q_in, k_st, decay_st))
    S0 = jnp.zeros((q.shape[0], B_HEADS, B_DK, B_DV), f32)
    _, o = lax.scan(step, S0, xs)
    o = from_chunks(jnp.moveaxis(o, 0, 2))
    zg = split_heads(z, B_HEADS).astype(f32)
    o = head_rms(o) * norm_w.astype(f32) * jax.nn.silu(zg)
    return o.reshape(o.shape[0], o.shape[1], B_HEADS * B_DV)


def rope(x, positions):
    half = x.shape[-1] // 2
    inv = ROPE_BASE ** (-jnp.arange(half, dtype=jnp.float32) / half)
    ang = positions.astype(jnp.float32)[..., None] * inv
    cos, sin = jnp.cos(ang)[:, :, None, :], jnp.sin(ang)[:, :, None, :]
    x1, x2 = x[..., :half], x[..., half:]
    return jnp.concatenate([x1 * cos - x2 * sin, x2 * cos + x1 * sin], axis=-1)


def retention_mixer(q, k, v, gate, positions):
    f32 = jnp.float32
    q = rope(split_heads(q, C_HEADS).astype(f32), positions)
    k = rope(split_heads(k, C_HEADS).astype(f32), positions) * (C_DK ** -0.5)
    v = split_heads(v, C_HEADS).astype(f32)
    log_gamma = jnp.log1p(-jnp.exp2(-5.0 - jnp.arange(C_HEADS, dtype=f32)))
    qc, kc, vc = to_chunks(q), to_chunks(k), to_chunks(v)
    idx = jnp.arange(CHUNK, dtype=f32)
    rel = idx[:, None] - idx[None, :]
    dmat = masked_exp(log_gamma[:, None, None] * rel, rel >= 0)
    scores = jnp.einsum('bhnik,bhnjk->bhnij', qc, kc) * dmat[:, None]
    o_intra = jnp.einsum('bhnij,bhnjv->bhniv', scores, vc)
    k_st = kc * jnp.exp(log_gamma[:, None] * (CHUNK - 1 - idx))[None, :, None, :, None]
    chunk_states = jnp.einsum('bhnjk,bhnjv->bhnkv', k_st, vc)
    decay_chunk = jnp.exp(log_gamma * CHUNK)[None, :, None, None]

    def step(R, s):
        return decay_chunk * R + s, R

    R0 = jnp.zeros((q.shape[0], C_HEADS, C_DK, C_DV), f32)
    _, r_prev = lax.scan(step, R0, jnp.moveaxis(chunk_states, 2, 0))
    r_prev = jnp.moveaxis(r_prev, 0, 2)
    q_in = qc * jnp.exp(log_gamma[:, None] * (idx + 1.0))[None, :, None, :, None]
    o = o_intra + jnp.einsum('bhnik,bhnkv->bhniv', q_in, r_prev)
    o = from_chunks(o)
    g = split_heads(gate, C_HEADS).astype(f32)
    o = head_rms(o) * jax.nn.silu(g)
    return o.reshape(o.shape[0], o.shape[1], C_HEADS * C_DV)


def setup_inputs(seed: int = 0) -> dict:
    key = jax.random.key(seed)
    ks = jax.random.split(key, 24)
    f32 = jnp.float32

    def nrm(k, shape, fan_in):
        return jax.random.normal(k, shape, f32) * (fan_in ** -0.5)

    def gain(k, shape):
        return 1.0 + 0.01 * jax.random.normal(k, shape, f32)

    x = jax.random.normal(ks[0], (BATCH, SEQ, D_MODEL), f32)
    offset = jax.random.randint(ks[1], (BATCH, 1), 0, 4096, dtype=jnp.int32)
    positions = jnp.arange(SEQ, dtype=jnp.int32)[None, :] + offset
    dt = jnp.exp(jax.random.uniform(ks[2], (DEPTH, B_HEADS), f32) * (math.log(0.1) - math.log(0.001)) + math.log(0.001))
    return {
        "x": x,
        "positions": positions,
        "ffn1_norm": gain(ks[3], (DEPTH, D_MODEL)),
        "ffn1_w_gate": nrm(ks[4], (DEPTH, D_MODEL, D_FF), D_MODEL),
        "ffn1_w_up": nrm(ks[5], (DEPTH, D_MODEL, D_FF), D_MODEL),
        "ffn1_w_down": nrm(ks[6], (DEPTH, D_FF, D_MODEL), D_FF),
        "mix_norm": gain(ks[7], (DEPTH, D_MODEL)),
        "w_in": nrm(ks[8], (DEPTH, D_MODEL, D_IN), D_MODEL),
        "hgrn_lower_bounds": 0.1 * jax.random.normal(ks[9], (DEPTH, A_HEADS * A_DK), f32),
        "hgrn_norm": gain(ks[10], (DEPTH, A_DV)),
        "gdn_conv": nrm(ks[11], (DEPTH, CONV_K, GDN_CONV_CH), CONV_K),
        "gdn_a_log": jnp.log(jax.random.uniform(ks[12], (DEPTH, B_HEADS), f32, 1.0, 16.0)),
        "gdn_dt_bias": dt + jnp.log(-jnp.expm1(-dt)),
        "gdn_norm": gain(ks[13], (DEPTH, B_DV)),
        "w_out": nrm(ks[14], (DEPTH, MIX_WIDTH, D_MODEL), MIX_WIDTH),
        "ffn2_norm": gain(ks[15], (DEPTH, D_MODEL)),
        "ffn2_w_gate": nrm(ks[16], (DEPTH, D_MODEL, D_FF), D_MODEL),
        "ffn2_w_up": nrm(ks[17], (DEPTH, D_MODEL, D_FF), D_MODEL),
        "ffn2_w_down": nrm(ks[18], (DEPTH, D_FF, D_MODEL), D_FF),
        "final_norm": gain(ks[19], (D_MODEL,)),
    }


def reference(x, positions, ffn1_norm, ffn1_w_gate, ffn1_w_up, ffn1_w_down, mix_norm, w_in,
              hgrn_lower_bounds, hgrn_norm, gdn_conv, gdn_a_log, gdn_dt_bias, gdn_norm, w_out,
              ffn2_norm, ffn2_w_gate, ffn2_w_up, ffn2_w_down, final_norm):
    lbs = hgrn2_lower_bounds(hgrn_lower_bounds)
    for l in range(DEPTH):
        h = rms_norm(x, ffn1_norm[l])
        x = x + 0.5 * swiglu_ffn(h, ffn1_w_gate[l], ffn1_w_up[l], ffn1_w_down[l])
        h = rms_norm(x, mix_norm[l])
        (a_q, a_f, a_i, a_g, b_q, b_k, b_v, b_z, b_beta, b_a,
         c_q, c_k, c_v, c_g) = jnp.split(h @ w_in[l], SPLIT_IDX, axis=-1)
        o_a = hgrn2_mixer(a_q, a_f, a_i, a_g, lbs[l], hgrn_norm[l])
        o_b = gated_deltanet_mixer(b_q, b_k, b_v, b_z, b_beta, b_a, gdn_conv[l], gdn_a_log[l], gdn_dt_bias[l], gdn_norm[l])
        o_c = retention_mixer(c_q, c_k, c_v, c_g, positions)
        mixed = jnp.concatenate([o_a, o_b, o_c], axis=-1).astype(x.dtype)
        x = x + mixed @ w_out[l]
        h = rms_norm(x, ffn2_norm[l])
        x = x + 0.5 * swiglu_ffn(h, ffn2_w_gate[l], ffn2_w_up[l], ffn2_w_down[l])
    return rms_norm(x, final_norm)
```

```python
import functools

import jax
import jax.numpy as jnp
import numpy as np
from jax import lax
from jax.experimental import pallas as pl
from jax.experimental.pallas import tpu as pltpu

F32 = jnp.float32
BF16 = jnp.bfloat16
HIGHEST = lax.Precision.HIGHEST

NORM_EPS = 1e-6
LOG_FLOOR = 1e-20
ROPE_BASE = 10000.0

A_HEADS, A_DK, A_DV = 4, 128, 64
B_HEADS, B_DK, B_DV = 4, 128, 128
C_HEADS, C_DK, C_DV = 4, 64, 64
CONV_K = 4
LANES = 128
SUBLANES = 8

CHUNK = 64
MIX_BLOCK = 256
ROW_BLOCK = 256
VMEM_LIMIT = 56 * 1024 * 1024

A_WIDTH = 2 * A_HEADS * A_DK + 2 * A_HEADS * A_DV
B_WIDTH = 2 * B_HEADS * B_DK + 2 * B_HEADS * B_DV
C_WIDTH = 2 * C_HEADS * C_DK + 2 * C_HEADS * C_DV
G_WIDTH = LANES


def _dot(a, b):
    return jnp.dot(a.astype(BF16), b.astype(BF16), preferred_element_type=F32)


def _dot_nt(a, b):
    return lax.dot_general(a.astype(BF16), b.astype(BF16), (((1,), (1,)), ((), ())),
                           preferred_element_type=F32)


def _dot_f32(a, b):
    return jnp.dot(a, b, precision=HIGHEST, preferred_element_type=F32)


def _sigmoid(x):
    return jax.nn.sigmoid(x)


def _silu(x):
    return x * jax.nn.sigmoid(x)


def _rms(x, w):
    return x * lax.rsqrt(jnp.mean(x * x, axis=-1, keepdims=True) + NORM_EPS) * w


def _const_spec(shape):
    nd = len(shape)
    return pl.BlockSpec(shape, lambda *_: (0,) * nd, pipeline_mode=pl.Buffered(1))


def _params(semantics):
    return pltpu.CompilerParams(dimension_semantics=semantics, vmem_limit_bytes=VMEM_LIMIT)


def _ffn_body(x, nw_ref, wg_ref, wu_ref, wd_ref):
    h = _rms(x, nw_ref[...]).astype(BF16)
    g = jnp.dot(h, wg_ref[...], preferred_element_type=F32)
    u = jnp.dot(h, wu_ref[...], preferred_element_type=F32)
    a = (_silu(g) * u).astype(BF16)
    return x + 0.5 * jnp.dot(a, wd_ref[...], preferred_element_type=F32)


def _ffn1_kernel(x_ref, nw_ref, wg_ref, wu_ref, wd_ref, o_ref):
    o_ref[...] = _ffn_body(x_ref[...], nw_ref, wg_ref, wu_ref, wd_ref)


def _ffn2_kernel(x_ref, oa_ref, ob_ref, oc_ref, woa_ref, wob_ref, woc_ref,
                 nw_ref, wg_ref, wu_ref, wd_ref, fw_ref, o_ref, *, final):
    x = x_ref[...]
    x = x + jnp.dot(oa_ref[...], woa_ref[...], preferred_element_type=F32)
    x = x + jnp.dot(ob_ref[...], wob_ref[...], preferred_element_type=F32)
    x = x + jnp.dot(oc_ref[...], woc_ref[...], preferred_element_type=F32)
    y = _ffn_body(x, nw_ref, wg_ref, wu_ref, wd_ref)
    if final:
        y = _rms(y, fw_ref[...])
    o_ref[...] = y


def _ffn1(x, nw, wg, wu, wd):
    t, d = x.shape
    row = pl.BlockSpec((ROW_BLOCK, d), lambda i: (i, 0))
    return pl.pallas_call(
        _ffn1_kernel,
        grid=(t // ROW_BLOCK,),
        in_specs=[row, _const_spec(nw.shape), _const_spec(wg.shape), _const_spec(wu.shape),
                  _const_spec(wd.shape)],
        out_specs=row,
        out_shape=jax.ShapeDtypeStruct((t, d), F32),
        compiler_params=_params(("parallel",)),
        name="ffn1",
    )(x, nw, wg, wu, wd)


def _ffn2(x, oa, ob, oc, woa, wob, woc, nw, wg, wu, wd, fw, final):
    t, d = x.shape

    def rows(a):
        return pl.BlockSpec((ROW_BLOCK, a.shape[1]), lambda i: (i, 0))

    consts = (woa, wob, woc, nw, wg, wu, wd, fw)
    return pl.pallas_call(
        functools.partial(_ffn2_kernel, final=final),
        grid=(t // ROW_BLOCK,),
        in_specs=[rows(x), rows(oa), rows(ob), rows(oc)] + [_const_spec(c.shape) for c in consts],
        out_specs=rows(x),
        out_shape=jax.ShapeDtypeStruct((t, d), F32),
        compiler_params=_params(("parallel",)),
        name="ffn2",
    )(x, oa, ob, oc, *consts)


def _proj_kernel(x_ref, nw_ref, w_ref, pa_ref, pb_ref, pc_ref, pg_ref):
    h = _rms(x_ref[...], nw_ref[...]).astype(BF16)
    p = jnp.dot(h, w_ref[...], preferred_element_type=F32)
    pa_ref[...] = p[:, :A_WIDTH]
    pb_ref[...] = p[:, A_WIDTH:A_WIDTH + B_WIDTH]
    pc_ref[...] = p[:, A_WIDTH + B_WIDTH:A_WIDTH + B_WIDTH + C_WIDTH]
    pg_ref[...] = p[:, A_WIDTH + B_WIDTH + C_WIDTH:]


def _proj(x, nw, w):
    t, d = x.shape
    widths = (A_WIDTH, B_WIDTH, C_WIDTH, G_WIDTH)
    return pl.pallas_call(
        _proj_kernel,
        grid=(t // ROW_BLOCK,),
        in_specs=[pl.BlockSpec((ROW_BLOCK, d), lambda i: (i, 0)), _const_spec(nw.shape),
                  _const_spec(w.shape)],
        out_specs=[pl.BlockSpec((ROW_BLOCK, n), lambda i: (i, 0)) for n in widths],
        out_shape=[jax.ShapeDtypeStruct((t, n), F32) for n in widths],
        compiler_params=_params(("parallel",)),
        name="in_proj",
    )(x, nw, w)


def _lower_bounds_kernel(p_ref, o_ref):
    p = p_ref[...]
    e = jnp.exp(p - jnp.max(p, axis=0, keepdims=True))
    s = e / jnp.sum(e, axis=0, keepdims=True)
    rows = [s[0:1]]
    for l in range(1, p.shape[0]):
        rows.append(rows[-1] + s[l:l + 1])
    o_ref[...] = jnp.concatenate(rows, axis=0) - s[0:1]


def _lower_bounds(p):
    return pl.pallas_call(_lower_bounds_kernel, out_shape=jax.ShapeDtypeStruct(p.shape, F32),
                          name="hgrn_lower_bounds")(p)


def _rope_kernel(pos_ref, inv_ref, cos_ref, sin_ref):
    ang = pos_ref[...].astype(F32) * inv_ref[...]
    lane = lax.broadcasted_iota(jnp.int32, ang.shape, 1)
    first_half = (lane % C_DK) < (C_DK // 2)
    cos_ref[...] = jnp.cos(ang)
    s = jnp.sin(ang)
    sin_ref[...] = jnp.where(first_half, -s, s)


def _rope_tables(pos, inv):
    t = pos.shape[0]
    out = jax.ShapeDtypeStruct((t, LANES), F32)
    blk = pl.BlockSpec((ROW_BLOCK, LANES), lambda i: (i, 0))
    return pl.pallas_call(
        _rope_kernel,
        grid=(t // ROW_BLOCK,),
        in_specs=[pl.BlockSpec((ROW_BLOCK, 1), lambda i: (i, 0)), _const_spec(inv.shape)],
        out_specs=[blk, blk],
        out_shape=[out, out],
        compiler_params=_params(("parallel",)),
        name="rope_tables",
    )(pos, inv)


def _iota2(shape, dim):
    return lax.broadcasted_iota(jnp.int32, shape, dim)


def _head_mean_square(o, width):
    n = o.shape[1]
    r = _iota2((n, n), 0) // width
    c = _iota2((n, n), 1) // width
    ones = jnp.where(r == c, 1.0 / width, 0.0).astype(BF16)
    sq = o * o
    hi = sq.astype(BF16)
    lo = (sq - hi.astype(F32)).astype(BF16)
    return (jnp.dot(hi, ones, preferred_element_type=F32)
            + jnp.dot(lo, ones, preferred_element_type=F32))


def _mixer_call(kernel, ins, in_specs, out_width, scratch, batch, seq, name):
    nblk = seq // MIX_BLOCK
    return pl.pallas_call(
        kernel,
        grid=(batch, nblk),
        in_specs=in_specs,
        out_specs=pl.BlockSpec((MIX_BLOCK, out_width), lambda b, s: (b * nblk + s, 0)),
        out_shape=jax.ShapeDtypeStruct((batch * seq, out_width), BF16),
        scratch_shapes=scratch,
        compiler_params=_params(("parallel", "arbitrary")),
        name=name,
    )(*ins)


def _seq_spec(width, nblk):
    return pl.BlockSpec((MIX_BLOCK, width), lambda b, s: (b * nblk + s, 0))


def _hgrn_intra(q, k, b):
    n = q.shape[0]
    row = _iota2((n, LANES), 0)
    r2 = _iota2((n, n), 0)
    c2 = _iota2((n, n), 1)
    attn = jnp.zeros((n, n), F32)
    half = n // 2
    while half >= SUBLANES:
        size = 2 * half
        ref = jnp.concatenate(
            [jnp.broadcast_to(b[s + half - 1:s + half, :], (size, LANES)) for s in range(0, n, size)],
            axis=0)
        upper = (row % size) >= half
        qd = jnp.where(upper, q * jnp.exp(jnp.minimum(b - ref, 0.0)), 0.0)
        kd = jnp.where(upper, 0.0, k * jnp.exp(jnp.minimum(ref - b, 0.0)))
        attn = attn + jnp.where((r2 // size) == (c2 // size), _dot_nt(qd, kd), 0.0)
        half //= 2
    for d in range(SUBLANES):
        if d == 0:
            p = q * k
        else:
            ks = pltpu.roll(k, d, 0)
            bs = pltpu.roll(b, d, 0)
            p = q * ks * jnp.exp(jnp.minimum(b - bs, 0.0))
        s = jnp.sum(p, axis=-1, keepdims=True)
        hit = ((r2 % SUBLANES) >= d) & (c2 == r2 - d)
        attn = attn + jnp.where(hit, s, 0.0)
    return attn


def _hgrn_kernel(pa_ref, lb_ref, nw_ref, o_ref, st_ref, raw_ref):
    @pl.when(pl.program_id(1) == 0)
    def _():
        st_ref[...] = jnp.zeros_like(st_ref)

    nk = A_HEADS * A_DK
    nv = A_HEADS * A_DV
    lb = lb_ref[...]
    r2 = _iota2((CHUNK, CHUNK), 0)
    c2 = _iota2((CHUNK, CHUNK), 1)
    tril = jnp.where(r2 >= c2, 1.0, 0.0).astype(F32)
    lane = _iota2((CHUNK, LANES), 1)

    def chunk(c, carry):
        r0 = pl.multiple_of(c * CHUNK, CHUNK)
        q = _silu(pa_ref[pl.ds(r0, CHUNK), 0:nk])
        z = pa_ref[pl.ds(r0, CHUNK), nk:2 * nk]
        v = pa_ref[pl.ds(r0, CHUNK), 2 * nk:2 * nk + nv]
        f = lb + (1.0 - lb) * _sigmoid(z)
        k = (1.0 - lb) * _sigmoid(-z)
        b = _dot_f32(tril, jnp.log(jnp.maximum(f, LOG_FLOOR)))
        b_last = b[CHUNK - 1:CHUNK, :]
        q_in = q * jnp.exp(b)
        k_st = k * jnp.exp(b_last - b)
        decay = jnp.exp(b_last)
        outs = []
        for pair in range(A_HEADS // 2):
            vp = v[:, pair * LANES:(pair + 1) * LANES]
            vpt = vp.T
            st = st_ref[pair]
            o_pair = jnp.zeros((CHUNK, LANES), F32)
            new_rows = []
            for sub in range(2):
                h = 2 * pair + sub
                sl = slice(h * A_DK, (h + 1) * A_DK)
                attn = _hgrn_intra(q[:, sl], k[:, sl], b[:, sl])
                o_h = _dot_nt(q_in[:, sl], st) + _dot(attn, vp)
                o_pair = jnp.where((lane // A_DV) == sub, o_h, o_pair)
                rows = slice(sub * A_DV, (sub + 1) * A_DV)
                new_rows.append(decay[:, sl] * st[rows, :] + _dot(vpt[rows, :], k_st[:, sl]))
            st_ref[pair] = jnp.concatenate(new_rows, axis=0)
            outs.append(o_pair)
        raw_ref[pl.ds(r0, CHUNK), :] = jnp.concatenate(outs, axis=1)
        return carry

    lax.fori_loop(0, MIX_BLOCK // CHUNK, chunk, 0)
    o = raw_ref[...]
    g = pa_ref[:, 2 * nk + nv:2 * nk + 2 * nv]
    o = o * lax.rsqrt(_head_mean_square(o, A_DV) + NORM_EPS) * nw_ref[...] * _silu(g)
    o_ref[...] = o.astype(BF16)


def _hgrn(pa, lb, nw, batch, seq):
    nblk = seq // MIX_BLOCK
    nv = A_HEADS * A_DV
    return _mixer_call(
        _hgrn_kernel, (pa, lb, nw),
        [_seq_spec(A_WIDTH, nblk), _const_spec(lb.shape), _const_spec(nw.shape)],
        nv,
        [pltpu.VMEM((A_HEADS // 2, 2 * A_DV, A_DK), F32), pltpu.VMEM((MIX_BLOCK, nv), F32)],
        batch, seq, "hgrn2")


def _unit_lower_inverse(m):
    n = m.shape[0]
    r = _iota2((n, n), 0)
    c = _iota2((n, n), 1)
    eye = jnp.where(r == c, 1.0, 0.0).astype(F32)
    inv = eye
    inner = 1
    while inner < n:
        outer = inner * 4
        part = jnp.where(((r // outer) == (c // outer)) & ((r // inner) != (c // inner)), m, 0.0)
        p = part if inner == 1 else _dot_f32(inv, part)
        corr = _dot_f32(eye - p, eye + _dot_f32(p, p))
        inv = corr if inner == 1 else _dot_f32(corr, inv)
        inner = outer
    return inv


def _softplus(x):
    return jnp.maximum(x, 0.0) + jnp.log1p(jnp.exp(-jnp.abs(x)))


def _gdn_kernel(pb_ref, pg_ref, cw_ref, alog_ref, dtb_ref, nw_ref, o_ref, st_ref, tail_ref):
    @pl.when(pl.program_id(1) == 0)
    def _():
        st_ref[...] = jnp.zeros_like(st_ref)
        tail_ref[...] = jnp.zeros_like(tail_ref)

    nqk = B_HEADS * B_DK
    ncv = 2 * nqk + B_HEADS * B_DV
    r2 = _iota2((CHUNK, CHUNK), 0)
    c2 = _iota2((CHUNK, CHUNK), 1)
    tril = jnp.where(r2 >= c2, 1.0, 0.0).astype(F32)
    row8 = _iota2((SUBLANES, ncv), 0)
    cw = cw_ref[...]
    neg_a = -jnp.exp(alog_ref[...])
    dtb = dtb_ref[...]
    nw = nw_ref[...]

    def chunk(c, carry):
        r0 = pl.multiple_of(c * CHUNK, CHUNK)
        x = pb_ref[pl.ds(r0, CHUNK), 0:ncv]
        prev = tail_ref[...]
        conv = x * cw[CONV_K - 1:CONV_K, :]
        for s in range(1, CONV_K):
            xs = pltpu.roll(x, s, 0)
            top = jnp.where(row8 < s, pltpu.roll(prev, s, 0), xs[0:SUBLANES, :])
            xs = jnp.concatenate([top, xs[SUBLANES:, :]], axis=0)
            conv = conv + xs * cw[CONV_K - 1 - s:CONV_K - s, :]
        tail_ref[...] = x[CHUNK - SUBLANES:, :]
        qkv = _silu(conv)

        pg = pg_ref[pl.ds(r0, CHUNK), :]
        beta_all = _sigmoid(pg)
        g_all = neg_a * _softplus(pg + dtb)
        gcum = _dot_f32(tril, g_all)
        gcum_t = gcum.T

        for h in range(B_HEADS):
            sl = slice(h * B_DK, (h + 1) * B_DK)
            q = qkv[:, sl]
            k = qkv[:, nqk + h * B_DK:nqk + (h + 1) * B_DK]
            v = qkv[:, 2 * nqk + h * B_DV:2 * nqk + (h + 1) * B_DV]
            q = q * lax.rsqrt(jnp.sum(q * q, axis=-1, keepdims=True) + NORM_EPS) * (B_DK ** -0.5)
            k = k * lax.rsqrt(jnp.sum(k * k, axis=-1, keepdims=True) + NORM_EPS)
            beta = jnp.broadcast_to(beta_all[:, h:h + 1], (CHUNK, LANES))
            gc = jnp.broadcast_to(gcum[:, B_HEADS + h:B_HEADS + h + 1], (CHUNK, LANES))
            gc_row = jnp.broadcast_to(gcum_t[B_HEADS + h:B_HEADS + h + 1, :], (CHUNK, CHUNK))
            gamma = jnp.exp(jnp.minimum(gc[:, 0:CHUNK] - gc_row, 0.0))
            m = jnp.where(r2 > c2, beta[:, 0:CHUNK] * _dot_nt(k, k) * gamma, 0.0)
            attn = jnp.where(r2 >= c2, _dot_nt(q, k) * gamma, 0.0)
            t = _unit_lower_inverse(m)
            e_gc = jnp.exp(gc)
            u = _dot_f32(t, v * beta)
            w = _dot_f32(t, k * (beta * e_gc))
            g_last = gc[CHUNK - 1:CHUNK, :]
            st = st_ref[h]
            v_new = u - _dot(w, st)
            o = _dot(q * e_gc, st) + _dot(attn, v_new)
            k_st = k * jnp.exp(g_last - gc)
            st_ref[h] = jnp.exp(g_last) * st + _dot(k_st.T, v_new)
            zg = pb_ref[pl.ds(r0, CHUNK), ncv + h * B_DV:ncv + (h + 1) * B_DV]
            o = o * lax.rsqrt(jnp.mean(o * o, axis=-1, keepdims=True) + NORM_EPS) * nw * _silu(zg)
            o_ref[pl.ds(r0, CHUNK), h * B_DV:(h + 1) * B_DV] = o.astype(BF16)
        return carry

    lax.fori_loop(0, MIX_BLOCK // CHUNK, chunk, 0)


def _gdn(pb, pg, cw, alog, dtb, nw, batch, seq):
    nblk = seq // MIX_BLOCK
    ncv = B_HEADS * (2 * B_DK + B_DV)
    consts = (cw, alog, dtb, nw)
    return _mixer_call(
        _gdn_kernel, (pb, pg) + consts,
        [_seq_spec(B_WIDTH, nblk), _seq_spec(G_WIDTH, nblk)] + [_const_spec(c.shape) for c in consts],
        B_HEADS * B_DV,
        [pltpu.VMEM((B_HEADS, B_DK, B_DV), F32), pltpu.VMEM((SUBLANES, ncv), F32)],
        batch, seq, "gated_deltanet")


def _ret_kernel(pc_ref, cos_ref, sin_ref, dmat_ref, qdec_ref, kdec_ref, sdec_ref, o_ref, st_ref):
    @pl.when(pl.program_id(1) == 0)
    def _():
        st_ref[...] = jnp.zeros_like(st_ref)

    nqk = C_HEADS * C_DK
    nv = C_HEADS * C_DV
    n = MIX_BLOCK
    lane = _iota2((n, LANES), 1)
    first_half = (lane % C_DK) < (C_DK // 2)
    cos = cos_ref[...]
    sin = sin_ref[...]
    r2 = _iota2((LANES, LANES), 0) // C_DK
    c2 = _iota2((LANES, LANES), 1) // C_DV

    def rope(x):
        rot = jnp.where(first_half, pltpu.roll(x, LANES - C_DK // 2, 1), pltpu.roll(x, C_DK // 2, 1))
        return x * cos + rot * sin

    outs = []
    for pair in range(C_HEADS // 2):
        sl = slice(pair * LANES, (pair + 1) * LANES)
        q = rope(pc_ref[:, sl])
        k = rope(pc_ref[:, nqk + pair * LANES:nqk + (pair + 1) * LANES]) * (C_DK ** -0.5)
        v = pc_ref[:, 2 * nqk + pair * LANES:2 * nqk + (pair + 1) * LANES]
        st = st_ref[pair]
        o_pair = _dot(q * qdec_ref[:, sl], st)
        for sub in range(2):
            h = 2 * pair + sub
            k_h = jnp.where((lane // C_DK) == sub, k, 0.0)
            scores = _dot_nt(q, k_h) * dmat_ref[h]
            o_h = _dot(scores, v)
            o_pair = o_pair + jnp.where((lane // C_DV) == sub, o_h, 0.0)
        ks = k * kdec_ref[:, sl]
        st_ref[pair] = sdec_ref[pair] * st + jnp.where(r2 == c2, _dot(ks.T, v), 0.0)
        outs.append(o_pair)
    o = jnp.concatenate(outs, axis=1)
    g = pc_ref[:, 2 * nqk + nv:2 * nqk + 2 * nv]
    o = o * lax.rsqrt(_head_mean_square(o, C_DV) + NORM_EPS) * _silu(g)
    o_ref[...] = o.astype(BF16)


def _retention_tables():
    n = MIX_BLOCK
    log_gamma = jnp.log1p(-jnp.exp2(-5.0 - jnp.arange(C_HEADS, dtype=F32)))
    idx = jnp.arange(n, dtype=F32)
    rel = idx[:, None] - idx[None, :]
    dmat = jnp.where(rel >= 0, jnp.exp(jnp.where(rel >= 0, log_gamma[:, None, None] * rel, 0.0)), 0.0)
    lane_gamma = jnp.repeat(log_gamma, C_DK)[None, :]
    qdec = jnp.exp(lane_gamma * (idx[:, None] + 1.0))
    kdec = jnp.exp(lane_gamma * (n - 1.0 - idx[:, None]))
    sdec = jnp.exp(lane_gamma * float(n)).reshape(C_HEADS // 2, LANES, 1)
    sdec = jnp.broadcast_to(sdec, (C_HEADS // 2, LANES, LANES))
    return dmat, qdec, kdec, sdec


def _retention(pc, cos, sin, tables, batch, seq):
    nblk = seq // MIX_BLOCK
    return _mixer_call(
        _ret_kernel, (pc, cos, sin) + tuple(tables),
        [_seq_spec(C_WIDTH, nblk), _seq_spec(LANES, nblk), _seq_spec(LANES, nblk)]
        + [_const_spec(t.shape) for t in tables],
        C_HEADS * C_DV,
        [pltpu.VMEM((C_HEADS // 2, LANES, LANES), F32)],
        batch, seq, "retention")


def _permute_w_in(w_in):
    na, nb = A_WIDTH, B_WIDTH
    gates = w_in[..., na + nb:na + nb + 2 * B_HEADS]
    rest = w_in[..., na + nb + 2 * B_HEADS:]
    pad = jnp.zeros(w_in.shape[:-1] + (G_WIDTH - 2 * B_HEADS,), w_in.dtype)
    return jnp.concatenate([w_in[..., :na + nb], rest, gates, pad], axis=-1)


def _lane_row(vals, offset):
    row = jnp.zeros((vals.shape[0], LANES), F32)
    return row.at[:, offset:offset + vals.shape[1]].set(vals.astype(F32))[:, None, :]


def kernel(x, positions, ffn1_norm, ffn1_w_gate, ffn1_w_up, ffn1_w_down, mix_norm, w_in,
           hgrn_lower_bounds, hgrn_norm, gdn_conv, gdn_a_log, gdn_dt_bias, gdn_norm, w_out,
           ffn2_norm, ffn2_w_gate, ffn2_w_up, ffn2_w_down, final_norm):
    batch, seq, d = x.shape
    depth = w_in.shape[0]
    t = batch * seq
    assert seq % MIX_BLOCK == 0 and t % ROW_BLOCK == 0

    bf = lambda w: w.astype(BF16)
    row = lambda w: w.astype(F32)[:, None, :]
    w_in_p = bf(_permute_w_in(w_in))
    wg1, wu1, wd1 = bf(ffn1_w_gate), bf(ffn1_w_up), bf(ffn1_w_down)
    wg2, wu2, wd2 = bf(ffn2_w_gate), bf(ffn2_w_up), bf(ffn2_w_down)
    w_out_b = bf(w_out)
    na, nb = A_HEADS * A_DV, B_HEADS * B_DV
    n1, m1, n2 = row(ffn1_norm), row(mix_norm), row(ffn2_norm)
    hgrn_nw = row(jnp.tile(hgrn_norm, (1, A_HEADS)))
    gdn_nw = row(gdn_norm)
    alog = _lane_row(gdn_a_log, B_HEADS)
    dtb = _lane_row(gdn_dt_bias, B_HEADS)
    fw = final_norm.astype(F32)[None, :]

    lbs = _lower_bounds(hgrn_lower_bounds.astype(F32))[:, None, :]
    half = C_DK // 2
    inv = ROPE_BASE ** (-jnp.arange(half, dtype=F32) / half)
    inv = jnp.tile(inv, LANES // half)[None, :]
    cos, sin = _rope_tables(positions.reshape(t, 1), inv)
    tables = _retention_tables()

    xt = x.reshape(t, d).astype(F32)
    for l in range(depth):
        xt = _ffn1(xt, n1[l], wg1[l], wu1[l], wd1[l])
        pa, pb, pc, pg = _proj(xt, m1[l], w_in_p[l])
        oa = _hgrn(pa, lbs[l], hgrn_nw[l], batch, seq)
        ob = _gdn(pb, pg, gdn_conv[l].astype(F32), alog[l], dtb[l], gdn_nw[l], batch, seq)
        oc = _retention(pc, cos, sin, tables, batch, seq)
        wo = w_out_b[l]
        xt = _ffn2(xt, oa, ob, oc, wo[:na], wo[na:na + nb], wo[na + nb:], n2[l], wg2[l], wu2[l],
                   wd2[l], fw, l == depth - 1)
    return xt.reshape(batch, seq, d).astype(x.dtype)
```

```python
import functools

import jax
import jax.numpy as jnp
import numpy as np
from jax import lax
from jax.experimental import pallas as pl
from jax.experimental.pallas import tpu as pltpu

F32 = jnp.float32
BF16 = jnp.bfloat16
HIGHEST = lax.Precision.HIGHEST

NORM_EPS = 1e-6
LOG_FLOOR = 1e-20
ROPE_BASE = 10000.0

A_HEADS, A_DK, A_DV = 4, 128, 64
B_HEADS, B_DK, B_DV = 4, 128, 128
C_HEADS, C_DK, C_DV = 4, 64, 64
CONV_K = 4
LANES = 128
SUBLANES = 8

CHUNK = 64
MIX_BLOCK = 256
ROW_BLOCK = 256
VMEM_LIMIT = 56 * 1024 * 1024

A_WIDTH = 2 * A_HEADS * A_DK + 2 * A_HEADS * A_DV
B_WIDTH = 2 * B_HEADS * B_DK + 2 * B_HEADS * B_DV
C_WIDTH = 2 * C_HEADS * C_DK + 2 * C_HEADS * C_DV
G_WIDTH = LANES


def _dot(a, b):
    return jnp.dot(a.astype(BF16), b.astype(BF16), preferred_element_type=F32)


def _dot_nt(a, b):
    return lax.dot_general(a.astype(BF16), b.astype(BF16), (((1,), (1,)), ((), ())),
                           preferred_element_type=F32)


def _dot_f32(a, b):
    return jnp.dot(a, b, precision=HIGHEST, preferred_element_type=F32)


def _sigmoid(x):
    return jax.nn.sigmoid(x)


def _silu(x):
    return x * jax.nn.sigmoid(x)


def _rms(x, w):
    return x * lax.rsqrt(jnp.mean(x * x, axis=-1, keepdims=True) + NORM_EPS) * w


def _const_spec(shape):
    nd = len(shape)
    return pl.BlockSpec(shape, lambda *_: (0,) * nd, pipeline_mode=pl.Buffered(1))


def _params(semantics):
    return pltpu.CompilerParams(dimension_semantics=semantics, vmem_limit_bytes=VMEM_LIMIT)


def _ffn_body(x, nw_ref, wg_ref, wu_ref, wd_ref):
    h = _rms(x, nw_ref[...]).astype(BF16)
    g = jnp.dot(h, wg_ref[...], preferred_element_type=F32)
    u = jnp.dot(h, wu_ref[...], preferred_element_type=F32)
    a = (_silu(g) * u).astype(BF16)
    return x + 0.5 * jnp.dot(a, wd_ref[...], preferred_element_type=F32)


def _ffn1_kernel(x_ref, nw_ref, wg_ref, wu_ref, wd_ref, o_ref):
    o_ref[...] = _ffn_body(x_ref[...], nw_ref, wg_ref, wu_ref, wd_ref)


def _ffn2_kernel(x_ref, oa_ref, ob_ref, oc_ref, woa_ref, wob_ref, woc_ref,
                 nw_ref, wg_ref, wu_ref, wd_ref, fw_ref, o_ref, *, final):
    x = x_ref[...]
    x = x + jnp.dot(oa_ref[...], woa_ref[...], preferred_element_type=F32)
    x = x + jnp.dot(ob_ref[...], wob_ref[...], preferred_element_type=F32)
    x = x + jnp.dot(oc_ref[...], woc_ref[...], preferred_element_type=F32)
    y = _ffn_body(x, nw_ref, wg_ref, wu_ref, wd_ref)
    if final:
        y = _rms(y, fw_ref[...])
    o_ref[...] = y


def _ffn1(x, nw, wg, wu, wd):
    t, d = x.shape
    row = pl.BlockSpec((ROW_BLOCK, d), lambda i: (i, 0))
    return pl.pallas_call(
        _ffn1_kernel,
        grid=(t // ROW_BLOCK,),
        in_specs=[row, _const_spec(nw.shape), _const_spec(wg.shape), _const_spec(wu.shape),
                  _const_spec(wd.shape)],
        out_specs=row,
        out_shape=jax.ShapeDtypeStruct((t, d), F32),
        compiler_params=_params(("parallel",)),
        name="ffn1",
    )(x, nw, wg, wu, wd)


def _ffn2(x, oa, ob, oc, woa, wob, woc, nw, wg, wu, wd, fw, final):
    t, d = x.shape

    def rows(a):
        return pl.BlockSpec((ROW_BLOCK, a.shape[1]), lambda i: (i, 0))

    consts = (woa, wob, woc, nw, wg, wu, wd, fw)
    return pl.pallas_call(
        functools.partial(_ffn2_kernel, final=final),
        grid=(t // ROW_BLOCK,),
        in_specs=[rows(x), rows(oa), rows(ob), rows(oc)] + [_const_spec(c.shape) for c in consts],
        out_specs=rows(x),
        out_shape=jax.ShapeDtypeStruct((t, d), F32),
        compiler_params=_params(("parallel",)),
        name="ffn2",
    )(x, oa, ob, oc, *consts)


def _proj_kernel(x_ref, nw_ref, w_ref, pa_ref, pb_ref, pc_ref, pg_ref):
    h = _rms(x_ref[...], nw_ref[...]).astype(BF16)
    p = jnp.dot(h, w_ref[...], preferred_element_type=F32)
    pa_ref[...] = p[:, :A_WIDTH]
    pb_ref[...] = p[:, A_WIDTH:A_WIDTH + B_WIDTH]
    pc_ref[...] = p[:, A_WIDTH + B_WIDTH:A_WIDTH + B_WIDTH + C_WIDTH]
    pg_ref[...] = p[:, A_WIDTH + B_WIDTH + C_WIDTH:]


def _proj(x, nw, w):
    t, d = x.shape
    widths = (A_WIDTH, B_WIDTH, C_WIDTH, G_WIDTH)
    return pl.pallas_call(
        _proj_kernel,
        grid=(t // ROW_BLOCK,),
        in_specs=[pl.BlockSpec((ROW_BLOCK, d), lambda i: (i, 0)), _const_spec(nw.shape),
                  _const_spec(w.shape)],
        out_specs=[pl.BlockSpec((ROW_BLOCK, n), lambda i: (i, 0)) for n in widths],
        out_shape=[jax.ShapeDtypeStruct((t, n), F32) for n in widths],
        compiler_params=_params(("parallel",)),
        name="in_proj",
    )(x, nw, w)


def _lower_bounds_kernel(p_ref, o_ref):
    p = p_ref[...]
    e = jnp.exp(p - jnp.max(p, axis=0, keepdims=True))
    s = e / jnp.sum(e, axis=0, keepdims=True)
    rows = [s[0:1]]
    for l in range(1, p.shape[0]):
        rows.append(rows[-1] + s[l:l + 1])
    o_ref[...] = jnp.concatenate(rows, axis=0) - s[0:1]


def _lower_bounds(p):
    return pl.pallas_call(_lower_bounds_kernel, out_shape=jax.ShapeDtypeStruct(p.shape, F32),
                          name="hgrn_lower_bounds")(p)


def _rope_kernel(pos_ref, inv_ref, cos_ref, sin_ref):
    ang = pos_ref[...].astype(F32) * inv_ref[...]
    lane = lax.broadcasted_iota(jnp.int32, ang.shape, 1)
    first_half = (lane % C_DK) < (C_DK // 2)
    cos_ref[...] = jnp.cos(ang)
    s = jnp.sin(ang)
    sin_ref[...] = jnp.where(first_half, -s, s)


def _rope_tables(pos, inv):
    t = pos.shape[0]
    out = jax.ShapeDtypeStruct((t, LANES), F32)
    blk = pl.BlockSpec((ROW_BLOCK, LANES), lambda i: (i, 0))
    return pl.pallas_call(
        _rope_kernel,
        grid=(t // ROW_BLOCK,),
        in_specs=[pl.BlockSpec((ROW_BLOCK, 1), lambda i: (i, 0)), _const_spec(inv.shape)],
        out_specs=[blk, blk],
        out_shape=[out, out],
        compiler_params=_params(("parallel",)),
        name="rope_tables",
    )(pos, inv)


def _iota2(shape, dim):
    return lax.broadcasted_iota(jnp.int32, shape, dim)


def _head_mean_square(o, width):
    n = o.shape[1]
    r = _iota2((n, n), 0) // width
    c = _iota2((n, n), 1) // width
    ones = jnp.where(r == c, 1.0 / width, 0.0).astype(BF16)
    sq = o * o
    hi = sq.astype(BF16)
    lo = (sq - hi.astype(F32)).astype(BF16)
    return (jnp.dot(hi, ones, preferred_element_type=F32)
            + jnp.dot(lo, ones, preferred_element_type=F32))


def _mixer_call(kernel, ins, in_specs, out_width, scratch, batch, seq, name):
    nblk = seq // MIX_BLOCK
    return pl.pallas_call(
        kernel,
        grid=(batch, nblk),
        in_specs=in_specs,
        out_specs=pl.BlockSpec((MIX_BLOCK, out_width), lambda b, s: (b * nblk + s, 0)),
        out_shape=jax.ShapeDtypeStruct((batch * seq, out_width), BF16),
        scratch_shapes=scratch,
        compiler_params=_params(("parallel", "arbitrary")),
        name=name,
    )(*ins)


def _seq_spec(width, nblk):
    return pl.BlockSpec((MIX_BLOCK, width), lambda b, s: (b * nblk + s, 0))


HGRN_LEVELS = tuple(CHUNK >> (i + 1) for i in range(CHUNK.bit_length() - 1))


def _hgrn_decay_weights():
    n = CHUNK
    i = np.arange(n)[:, None]
    t = np.arange(n)[None, :]
    blocks = [t <= i, t > i]
    for h in HGRN_LEVELS:
        ref = (i // (2 * h)) * 2 * h + h - 1
        upper = (i % (2 * h)) >= h
        blocks.append(np.where(upper, (t > ref) & (t <= i), (t > i) & (t <= ref)))
    w = np.concatenate(blocks, axis=0).astype(np.float32)
    return jnp.asarray(np.concatenate([w, w, w], axis=1), dtype=BF16)


def _hgrn_kernel(pa_ref, lb_ref, nw_ref, w_ref, o_ref, st_ref, raw_ref):
    @pl.when(pl.program_id(1) == 0)
    def _():
        st_ref[...] = jnp.zeros_like(st_ref)

    nk = A_HEADS * A_DK
    nv = A_HEADS * A_DV
    heads = range(A_HEADS)
    nchunks = MIX_BLOCK // CHUNK
    lb = lb_ref[...]
    r2 = _iota2((CHUNK, CHUNK), 0)
    c2 = _iota2((CHUNK, CHUNK), 1)
    row = _iota2((CHUNK, LANES), 0)
    lane = _iota2((CHUNK, LANES), 1)
    uppers = [(row % (2 * h)) >= h for h in HGRN_LEVELS]
    masks = [((r2 // (2 * h)) == (c2 // (2 * h))) & ((r2 % (2 * h)) >= h) & ((c2 % (2 * h)) < h)
             for h in HGRN_LEVELS]

    attns, q_ins, k_sts, decays = [], [], [], []
    for ci in range(nchunks):
        r0 = ci * CHUNK
        q = _silu(pa_ref[r0:r0 + CHUNK, 0:nk])
        z = pa_ref[r0:r0 + CHUNK, nk:2 * nk]
        f = lb + (1.0 - lb) * _sigmoid(z)
        k = (1.0 - lb) * _sigmoid(-z)
        l1f = jnp.log2(jnp.maximum(f, LOG_FLOOR))
        l1 = l1f.astype(BF16)
        rem = l1f - l1.astype(F32)
        l2 = rem.astype(BF16)
        l3 = (rem - l2.astype(F32)).astype(BF16)
        e = jnp.exp2(_bdot(w_ref[...], jnp.concatenate([l1, l2, l3], axis=0)))
        q_ins.append((q * e[0:CHUNK]).astype(BF16))
        k_sts.append((k * e[CHUNK:2 * CHUNK]).astype(BF16))
        decays.append(e[CHUNK - 1:CHUNK])
        for h in heads:
            sl = slice(h * A_DK, (h + 1) * A_DK)
            q_h, k_h = q[:, sl], k[:, sl]
            attn = jnp.where(r2 == c2, jnp.sum(q_h * k_h, axis=-1, keepdims=True), 0.0)
            for lv in range(len(HGRN_LEVELS)):
                e_lv = e[(2 + lv) * CHUNK:(3 + lv) * CHUNK, sl]
                x = (jnp.where(uppers[lv], q_h, k_h) * e_lv).astype(BF16)
                p = lax.dot_general(x, x, (((1,), (1,)), ((), ())), preferred_element_type=F32)
                attn = attn + jnp.where(masks[lv], p, 0.0)
            attns.append(attn.astype(BF16))

    sts = [st_ref[pair] for pair in range(A_HEADS // 2)]
    for ci in range(nchunks):
        r0 = ci * CHUNK
        outs = []
        for pair in range(A_HEADS // 2):
            vp = pa_ref[r0:r0 + CHUNK, 2 * nk + pair * LANES:2 * nk + (pair + 1) * LANES]
            vpb = vp.astype(BF16)
            vpt = vp.T.astype(BF16)
            st = sts[pair]
            stb = st.astype(BF16)
            o_pair = jnp.zeros((CHUNK, LANES), F32)
            new_rows = []
            for sub in range(2):
                h = 2 * pair + sub
                sl = slice(h * A_DK, (h + 1) * A_DK)
                o_h = (lax.dot_general(q_ins[ci][:, sl], stb, (((1,), (1,)), ((), ())),
                                       preferred_element_type=F32)
                       + _bdot(attns[ci * A_HEADS + h], vpb))
                o_pair = jnp.where((lane // A_DV) == sub, o_h, o_pair)
                rows = slice(sub * A_DV, (sub + 1) * A_DV)
                new_rows.append(decays[ci][:, sl] * st[rows, :] + _bdot(vpt[rows, :], k_sts[ci][:, sl]))
            sts[pair] = jnp.concatenate(new_rows, axis=0)
            outs.append(o_pair)
        raw_ref[r0:r0 + CHUNK, :] = jnp.concatenate(outs, axis=1)
    for pair in range(A_HEADS // 2):
        st_ref[pair] = sts[pair]

    o = raw_ref[...]
    g = pa_ref[:, 2 * nk + nv:2 * nk + 2 * nv]
    o = o * lax.rsqrt(_head_mean_square(o, A_DV) + NORM_EPS) * nw_ref[...] * _silu(g)
    o_ref[...] = o.astype(BF16)


def _hgrn(pa, lb, nw, batch, seq):
    nblk = seq // MIX_BLOCK
    nv = A_HEADS * A_DV
    w = _hgrn_decay_weights()
    return _mixer_call(
        _hgrn_kernel, (pa, lb, nw, w),
        [_seq_spec(A_WIDTH, nblk), _const_spec(lb.shape), _const_spec(nw.shape), _const_spec(w.shape)],
        nv,
        [pltpu.VMEM((A_HEADS // 2, 2 * A_DV, A_DK), F32), pltpu.VMEM((MIX_BLOCK, nv), F32)],
        batch, seq, "hgrn2")


def _split(a):
    hi = a.astype(BF16)
    lo = (a - hi.astype(F32)).astype(BF16)
    return hi, lo


def _packed_operands(a):
    n, w = a.shape
    hi = a.astype(BF16)
    hi_f = hi.astype(F32)
    lo_f = a - hi_f
    lo = lo_f.astype(BF16)
    col = jnp.where(_iota2((n, w), 1) < w // 2, hi_f, lo_f).astype(BF16)
    return jnp.concatenate([col, col], axis=1), jnp.concatenate([hi, hi, lo, lo], axis=0)


def _packed_rhs(b):
    hi, lo = _split(b)
    return jnp.concatenate([hi, hi, lo, lo], axis=0)


def _bdot(a, b):
    return jnp.dot(a, b, preferred_element_type=F32)


def _unit_lower_inverses(ms):
    n = ms[0].shape[0]
    r = _iota2((n, 2 * n), 0)
    c = _iota2((n, 2 * n), 1) % n
    eye = jnp.where(r == c, 1.0, 0.0).astype(F32)
    inv_ops = None
    inner = 1
    while inner < n:
        outer = inner * 4
        mask = ((r // outer) == (c // outer)) & ((r // inner) != (c // inner))
        ps = [jnp.where(mask, m, 0.0) for m in ms]
        if inv_ops is not None:
            ps = [_bdot(inv[0], _packed_rhs(p)) for inv, p in zip(inv_ops, ps)]
        p_ops = [_packed_operands(p) for p in ps]
        pps = [_bdot(po[0], po[1]) for po in p_ops]
        corrs = [_bdot(_packed_operands(eye - p)[0], _packed_rhs(eye + pp)) for p, pp in zip(ps, pps)]
        if inv_ops is not None:
            corrs = [_bdot(_packed_operands(cr)[0], inv[1]) for cr, inv in zip(corrs, inv_ops)]
        inv_ops = [_packed_operands(cr) for cr in corrs]
        inner = outer
    return [inv[0] for inv in inv_ops]


def _softplus(x):
    return jnp.maximum(x, 0.0) + jnp.log1p(jnp.exp(-jnp.abs(x)))


def _gdn_kernel(pb_ref, pg_ref, cw_ref, alog_ref, dtb_ref, nw_ref, o_ref, st_ref, tail_ref):
    @pl.when(pl.program_id(1) == 0)
    def _():
        st_ref[...] = jnp.zeros_like(st_ref)
        tail_ref[...] = jnp.zeros_like(tail_ref)

    nqk = B_HEADS * B_DK
    ncv = 2 * nqk + B_HEADS * B_DV
    heads = range(B_HEADS)
    r2 = _iota2((CHUNK, CHUNK), 0)
    c2 = _iota2((CHUNK, CHUNK), 1)
    rd = _iota2((CHUNK, 2 * CHUNK), 0)
    cd = _iota2((CHUNK, 2 * CHUNK), 1) % CHUNK
    tril3 = jnp.where(_iota2((CHUNK, 3 * CHUNK), 0) >= _iota2((CHUNK, 3 * CHUNK), 1) % CHUNK,
                      1.0, 0.0).astype(BF16)
    row8 = _iota2((SUBLANES, ncv), 0)
    cw = cw_ref[...]
    neg_a = -jnp.exp(alog_ref[...])
    dtb = dtb_ref[...]
    nw = nw_ref[...]

    nchunks = MIX_BLOCK // CHUNK
    qes, kts, decays, ms, attns, rhss = [], [], [], [], [], []
    for ci in range(nchunks):
        r0 = ci * CHUNK
        x = pb_ref[r0:r0 + CHUNK, 0:ncv]
        prev = tail_ref[...] if ci == 0 else pb_ref[r0 - SUBLANES:r0, 0:ncv]
        conv = x * cw[CONV_K - 1:CONV_K, :]
        for s in range(1, CONV_K):
            xs = pltpu.roll(x, s, 0)
            top = jnp.where(row8 < s, pltpu.roll(prev, s, 0), xs[0:SUBLANES, :])
            xs = jnp.concatenate([top, xs[SUBLANES:, :]], axis=0)
            conv = conv + xs * cw[CONV_K - 1 - s:CONV_K - s, :]
        qkv = _silu(conv)

        pg = pg_ref[r0:r0 + CHUNK, :]
        beta_all = _sigmoid(pg)
        g_all = neg_a * _softplus(pg + dtb)
        g1 = g_all.astype(BF16)
        g_rem = g_all - g1.astype(F32)
        g2 = g_rem.astype(BF16)
        g3 = (g_rem - g2.astype(F32)).astype(BF16)
        gcum = _bdot(tril3, jnp.concatenate([g1, g2, g3], axis=0))
        gcum_t = jnp.concatenate([gcum, gcum], axis=0).T

        for h in heads:
            q = qkv[:, h * B_DK:(h + 1) * B_DK]
            k = qkv[:, nqk + h * B_DK:nqk + (h + 1) * B_DK]
            v = qkv[:, 2 * nqk + h * B_DV:2 * nqk + (h + 1) * B_DV]
            q = q * lax.rsqrt(jnp.sum(q * q, axis=-1, keepdims=True) + NORM_EPS) * (B_DK ** -0.5)
            k = k * lax.rsqrt(jnp.sum(k * k, axis=-1, keepdims=True) + NORM_EPS)
            beta = jnp.broadcast_to(beta_all[:, h:h + 1], (CHUNK, LANES))
            gc = jnp.broadcast_to(gcum[:, B_HEADS + h:B_HEADS + h + 1], (CHUNK, LANES))
            gc_row = jnp.broadcast_to(gcum_t[B_HEADS + h:B_HEADS + h + 1, :], (CHUNK, 2 * CHUNK))
            gamma = jnp.exp(jnp.minimum(gc - gc_row, 0.0))
            kb = k.astype(BF16)
            kk = lax.dot_general(kb, jnp.concatenate([kb, kb], axis=0), (((1,), (1,)), ((), ())),
                                 preferred_element_type=F32)
            ms.append(jnp.where(rd > cd, beta * kk * gamma, 0.0))
            attns.append(jnp.where(r2 >= c2, _dot_nt(q, kb) * gamma[:, 0:CHUNK], 0.0).astype(BF16))
            e_gc = jnp.exp(gc)
            rhss.append(_packed_rhs(jnp.concatenate([v * beta, k * (beta * e_gc)], axis=1)))
            g_last = gc[CHUNK - 1:CHUNK, :]
            qes.append((q * e_gc).astype(BF16))
            kts.append((k * jnp.exp(g_last - gc)).T.astype(BF16))
            decays.append(jnp.exp(g_last))
    tail_ref[...] = pb_ref[MIX_BLOCK - SUBLANES:MIX_BLOCK, 0:ncv]

    t_ops = _unit_lower_inverses(ms)
    uws = [_bdot(t, rhs) for t, rhs in zip(t_ops, rhss)]

    sts = [st_ref[h] for h in heads]
    for ci in range(nchunks):
        r0 = ci * CHUNK
        idx = [ci * B_HEADS + h for h in heads]
        st_bs = [st.astype(BF16) for st in sts]
        v_news = [(uws[i][:, 0:B_DV] - _bdot(uws[i][:, B_DV:].astype(BF16), sb)).astype(BF16)
                  for i, sb in zip(idx, st_bs)]
        outs = [_bdot(qes[i], sb) + _bdot(attns[i], vn) for i, sb, vn in zip(idx, st_bs, v_news)]
        sts = [decays[i] * st + _bdot(kts[i], vn) for i, st, vn in zip(idx, sts, v_news)]
        for h in heads:
            o = outs[h]
            zg = pb_ref[r0:r0 + CHUNK, ncv + h * B_DV:ncv + (h + 1) * B_DV]
            o = o * lax.rsqrt(jnp.mean(o * o, axis=-1, keepdims=True) + NORM_EPS) * nw * _silu(zg)
            o_ref[r0:r0 + CHUNK, h * B_DV:(h + 1) * B_DV] = o.astype(BF16)
    for h in heads:
        st_ref[h] = sts[h]


def _gdn(pb, pg, cw, alog, dtb, nw, batch, seq):
    nblk = seq // MIX_BLOCK
    ncv = B_HEADS * (2 * B_DK + B_DV)
    consts = (cw, alog, dtb, nw)
    return _mixer_call(
        _gdn_kernel, (pb, pg) + consts,
        [_seq_spec(B_WIDTH, nblk), _seq_spec(G_WIDTH, nblk)] + [_const_spec(c.shape) for c in consts],
        B_HEADS * B_DV,
        [pltpu.VMEM((B_HEADS, B_DK, B_DV), F32), pltpu.VMEM((SUBLANES, ncv), F32)],
        batch, seq, "gated_deltanet")


def _ret_kernel(pc_ref, cos_ref, sin_ref, dmat_ref, qdec_ref, kdec_ref, sdec_ref, o_ref, st_ref):
    @pl.when(pl.program_id(1) == 0)
    def _():
        st_ref[...] = jnp.zeros_like(st_ref)

    nqk = C_HEADS * C_DK
    nv = C_HEADS * C_DV
    n = MIX_BLOCK
    lane = _iota2((n, LANES), 1)
    first_half = (lane % C_DK) < (C_DK // 2)
    cos = cos_ref[...]
    sin = sin_ref[...]
    r2 = _iota2((LANES, LANES), 0) // C_DK
    c2 = _iota2((LANES, LANES), 1) // C_DV

    def rope(x):
        rot = jnp.where(first_half, pltpu.roll(x, LANES - C_DK // 2, 1), pltpu.roll(x, C_DK // 2, 1))
        return x * cos + rot * sin

    outs = []
    for pair in range(C_HEADS // 2):
        sl = slice(pair * LANES, (pair + 1) * LANES)
        q = rope(pc_ref[:, sl])
        k = rope(pc_ref[:, nqk + pair * LANES:nqk + (pair + 1) * LANES]) * (C_DK ** -0.5)
        v = pc_ref[:, 2 * nqk + pair * LANES:2 * nqk + (pair + 1) * LANES]
        st = st_ref[pair]
        o_pair = _dot(q * qdec_ref[:, sl], st)
        for sub in range(2):
            h = 2 * pair + sub
            k_h = jnp.where((lane // C_DK) == sub, k, 0.0)
            scores = _dot_nt(q, k_h) * dmat_ref[h]
            o_h = _dot(scores, v)
            o_pair = o_pair + jnp.where((lane // C_DV) == sub, o_h, 0.0)
        ks = k * kdec_ref[:, sl]
        st_ref[pair] = sdec_ref[pair] * st + jnp.where(r2 == c2, _dot(ks.T, v), 0.0)
        outs.append(o_pair)
    o = jnp.concatenate(outs, axis=1)
    g = pc_ref[:, 2 * nqk + nv:2 * nqk + 2 * nv]
    o = o * lax.rsqrt(_head_mean_square(o, C_DV) + NORM_EPS) * _silu(g)
    o_ref[...] = o.astype(BF16)


def _retention_tables():
    n = MIX_BLOCK
    log_gamma = jnp.log1p(-jnp.exp2(-5.0 - jnp.arange(C_HEADS, dtype=F32)))
    idx = jnp.arange(n, dtype=F32)
    rel = idx[:, None] - idx[None, :]
    dmat = jnp.where(rel >= 0, jnp.exp(jnp.where(rel >= 0, log_gamma[:, None, None] * rel, 0.0)), 0.0)
    lane_gamma = jnp.repeat(log_gamma, C_DK)[None, :]
    qdec = jnp.exp(lane_gamma * (idx[:, None] + 1.0))
    kdec = jnp.exp(lane_gamma * (n - 1.0 - idx[:, None]))
    sdec = jnp.exp(lane_gamma * float(n)).reshape(C_HEADS // 2, LANES, 1)
    sdec = jnp.broadcast_to(sdec, (C_HEADS // 2, LANES, LANES))
    return dmat, qdec, kdec, sdec


def _retention(pc, cos, sin, tables, batch, seq):
    nblk = seq // MIX_BLOCK
    return _mixer_call(
        _ret_kernel, (pc, cos, sin) + tuple(tables),
        [_seq_spec(C_WIDTH, nblk), _seq_spec(LANES, nblk), _seq_spec(LANES, nblk)]
        + [_const_spec(t.shape) for t in tables],
        C_HEADS * C_DV,
        [pltpu.VMEM((C_HEADS // 2, LANES, LANES), F32)],
        batch, seq, "retention")


def _permute_w_in(w_in):
    na, nb = A_WIDTH, B_WIDTH
    gates = w_in[..., na + nb:na + nb + 2 * B_HEADS]
    rest = w_in[..., na + nb + 2 * B_HEADS:]
    pad = jnp.zeros(w_in.shape[:-1] + (G_WIDTH - 2 * B_HEADS,), w_in.dtype)
    return jnp.concatenate([w_in[..., :na + nb], rest, gates, pad], axis=-1)


def _lane_row(vals, offset):
    row = jnp.zeros((vals.shape[0], LANES), F32)
    return row.at[:, offset:offset + vals.shape[1]].set(vals.astype(F32))[:, None, :]


def kernel(x, positions, ffn1_norm, ffn1_w_gate, ffn1_w_up, ffn1_w_down, mix_norm, w_in,
           hgrn_lower_bounds, hgrn_norm, gdn_conv, gdn_a_log, gdn_dt_bias, gdn_norm, w_out,
           ffn2_norm, ffn2_w_gate, ffn2_w_up, ffn2_w_down, final_norm):
    batch, seq, d = x.shape
    depth = w_in.shape[0]
    t = batch * seq
    assert seq % MIX_BLOCK == 0 and t % ROW_BLOCK == 0

    bf = lambda w: w.astype(BF16)
    row = lambda w: w.astype(F32)[:, None, :]
    w_in_p = bf(_permute_w_in(w_in))
    wg1, wu1, wd1 = bf(ffn1_w_gate), bf(ffn1_w_up), bf(ffn1_w_down)
    wg2, wu2, wd2 = bf(ffn2_w_gate), bf(ffn2_w_up), bf(ffn2_w_down)
    w_out_b = bf(w_out)
    na, nb = A_HEADS * A_DV, B_HEADS * B_DV
    n1, m1, n2 = row(ffn1_norm), row(mix_norm), row(ffn2_norm)
    hgrn_nw = row(jnp.tile(hgrn_norm, (1, A_HEADS)))
    gdn_nw = row(gdn_norm)
    alog = _lane_row(gdn_a_log, B_HEADS)
    dtb = _lane_row(gdn_dt_bias, B_HEADS)
    fw = final_norm.astype(F32)[None, :]

    lbs = _lower_bounds(hgrn_lower_bounds.astype(F32))[:, None, :]
    half = C_DK // 2
    inv = ROPE_BASE ** (-jnp.arange(half, dtype=F32) / half)
    inv = jnp.tile(inv, LANES // half)[None, :]
    cos, sin = _rope_tables(positions.reshape(t, 1), inv)
    tables = _retention_tables()

    xt = x.reshape(t, d).astype(F32)
    for l in range(depth):
        xt = _ffn1(xt, n1[l], wg1[l], wu1[l], wd1[l])
        pa, pb, pc, pg = _proj(xt, m1[l], w_in_p[l])
        oa = _hgrn(pa, lbs[l], hgrn_nw[l], batch, seq)
        ob = _gdn(pb, pg, gdn_conv[l].astype(F32), alog[l], dtb[l], gdn_nw[l], batch, seq)
        oc = _retention(pc, cos, sin, tables, batch, seq)
        wo = w_out_b[l]
        xt = _ffn2(xt, oa, ob, oc, wo[:na], wo[na:na + nb], wo[na + nb:], n2[l], wg2[l], wu2[l],
                   wd2[l], fw, l == depth - 1)
    return xt.reshape(batch, seq, d).astype(x.dtype)
```

```python
import functools

import jax
import jax.numpy as jnp
import numpy as np
from jax import lax
from jax.experimental import pallas as pl
from jax.experimental.pallas import tpu as pltpu

F32 = jnp.float32
BF16 = jnp.bfloat16

NORM_EPS = 1e-6
LOG_FLOOR = 1e-20
ROPE_BASE = 10000.0

A_HEADS, A_DK, A_DV = 4, 128, 64
B_HEADS, B_DK, B_DV = 4, 128, 128
C_HEADS, C_DK, C_DV = 4, 64, 64
CONV_K = 4
LANES = 128
SUBLANES = 8

CHUNK = 64
MIX_BLOCK = 256
ROW_BLOCK = 256
PROJ_TILE = 256
VMEM_LIMIT = 56 * 1024 * 1024

A_WIDTH = 2 * A_HEADS * A_DK + 2 * A_HEADS * A_DV
B_WIDTH = 2 * B_HEADS * B_DK + 2 * B_HEADS * B_DV
C_WIDTH = 2 * C_HEADS * C_DK + 2 * C_HEADS * C_DV
G_WIDTH = LANES


def _dot(a, b):
    return jnp.dot(a.astype(BF16), b.astype(BF16), preferred_element_type=F32)


def _dot_nt(a, b):
    return lax.dot_general(a.astype(BF16), b.astype(BF16), (((1,), (1,)), ((), ())),
                           preferred_element_type=F32)


def _bdot(a, b):
    return jnp.dot(a, b, preferred_element_type=F32)


def _iota2(shape, dim):
    return lax.broadcasted_iota(jnp.int32, shape, dim)


def _softplus(x):
    return jnp.maximum(x, 0.0) + jnp.log1p(jnp.exp(-jnp.abs(x)))


def _split(a):
    hi = a.astype(BF16)
    lo = (a - hi.astype(F32)).astype(BF16)
    return hi, lo


def _split3(a):
    t1 = a.astype(BF16)
    rem = a - t1.astype(F32)
    t2 = rem.astype(BF16)
    return t1, t2, (rem - t2.astype(F32)).astype(BF16)


def _sigmoid(x):
    return jax.nn.sigmoid(x)


def _silu(x):
    return x * jax.nn.sigmoid(x)


def _rms(x, w):
    return x * lax.rsqrt(jnp.mean(x * x, axis=-1, keepdims=True) + NORM_EPS) * w


def _const_spec(shape):
    nd = len(shape)
    return pl.BlockSpec(shape, lambda *_: (0,) * nd, pipeline_mode=pl.Buffered(1))


def _layer_spec(shape, layer):
    return pl.BlockSpec((None,) + tuple(shape[1:]), lambda *_: (layer, 0, 0), pipeline_mode=pl.Buffered(1))


def _params(semantics):
    return pltpu.CompilerParams(dimension_semantics=semantics, vmem_limit_bytes=VMEM_LIMIT)


def _ffn_body(x, nw_ref, wg_ref, wu_ref, wd_ref):
    h = _rms(x, nw_ref[...]).astype(BF16)
    g = jnp.dot(h, wg_ref[...], preferred_element_type=F32)
    u = jnp.dot(h, wu_ref[...], preferred_element_type=F32)
    a = (_silu(g) * u).astype(BF16)
    return x + 0.5 * jnp.dot(a, wd_ref[...], preferred_element_type=F32)


def _ffn1_kernel(x_ref, nw_ref, wg_ref, wu_ref, wd_ref, o_ref):
    o_ref[...] = _ffn_body(x_ref[...], nw_ref, wg_ref, wu_ref, wd_ref)


def _ffn2_kernel(x_ref, oa_ref, ob_ref, oc_ref, wo_ref, nw_ref, wg_ref, wu_ref, wd_ref, fw_ref, o_ref,
                 *, final):
    na, nb = oa_ref.shape[1], ob_ref.shape[1]
    x = x_ref[...]
    x = x + jnp.dot(oa_ref[...], wo_ref[0:na, :], preferred_element_type=F32)
    x = x + jnp.dot(ob_ref[...], wo_ref[na:na + nb, :], preferred_element_type=F32)
    x = x + jnp.dot(oc_ref[...], wo_ref[na + nb:, :], preferred_element_type=F32)
    y = _ffn_body(x, nw_ref, wg_ref, wu_ref, wd_ref)
    if final:
        y = _rms(y, fw_ref[...])
    o_ref[...] = y


def _ffn1(x, layer, nw, wg, wu, wd):
    t, d = x.shape
    row = pl.BlockSpec((ROW_BLOCK, d), lambda i: (i, 0))
    return pl.pallas_call(
        _ffn1_kernel,
        grid=(t // ROW_BLOCK,),
        in_specs=[row] + [_layer_spec(c.shape, layer) for c in (nw, wg, wu, wd)],
        out_specs=row,
        out_shape=jax.ShapeDtypeStruct((t, d), F32),
        compiler_params=_params(("parallel",)),
        name="ffn1",
    )(x, nw, wg, wu, wd)


def _ffn2(x, oa, ob, oc, layer, wo, nw, wg, wu, wd, fw, final):
    t, d = x.shape

    def rows(a):
        return pl.BlockSpec((ROW_BLOCK, a.shape[1]), lambda i: (i, 0))

    consts = (wo, nw, wg, wu, wd)
    return pl.pallas_call(
        functools.partial(_ffn2_kernel, final=final),
        grid=(t // ROW_BLOCK,),
        in_specs=([rows(x), rows(oa), rows(ob), rows(oc)] + [_layer_spec(c.shape, layer) for c in consts]
                  + [_const_spec(fw.shape)]),
        out_specs=rows(x),
        out_shape=jax.ShapeDtypeStruct((t, d), F32),
        compiler_params=_params(("parallel",)),
        name="ffn2",
    )(x, oa, ob, oc, *consts, fw)


def _proj_kernel(x_ref, nw_ref, w_ref, cw_ref, lb_ref, alog_ref, dtb_ref, cos_ref, sin_ref,
                 pa_ref, al_ref, pb_ref, pg_ref, pc_ref, tail_ref, *, nblk):
    @pl.when(pl.program_id(0) % nblk == 0)
    def _():
        tail_ref[...] = jnp.zeros_like(tail_ref)

    n = ROW_BLOCK
    h = _rms(x_ref[...], nw_ref[...]).astype(BF16)
    off_b = A_WIDTH
    off_c = A_WIDTH + B_WIDTH
    off_g = A_WIDTH + B_WIDTH + C_WIDTH
    tile = PROJ_TILE

    def project(c0, width=tile):
        return _bdot(h, w_ref[:, c0:c0 + width])

    heavy, light = [], []

    def gates():
        pg = project(off_g, G_WIDTH)
        log_decay = -jnp.exp(alog_ref[...]) * _softplus(pg + dtb_ref[...])
        pg_ref[...] = jnp.where(_iota2((n, G_WIDTH), 1) < B_HEADS, _sigmoid(pg), log_decay)

    light.append(gates)

    nqk = B_HEADS * B_DK
    ncv = 2 * nqk + B_HEADS * B_DV
    row8 = _iota2((SUBLANES, tile), 0)

    def conv_tile(c0):
        x = project(off_b + c0)
        cw = cw_ref[:, c0:c0 + tile]
        prev = tail_ref[:, c0:c0 + tile]
        conv = x * cw[CONV_K - 1:CONV_K, :]
        for s in range(1, CONV_K):
            xs = pltpu.roll(x, s, 0)
            top = jnp.where(row8 < s, pltpu.roll(prev, s, 0), xs[0:SUBLANES, :])
            xs = jnp.concatenate([top, xs[SUBLANES:, :]], axis=0)
            conv = conv + xs * cw[CONV_K - 1 - s:CONV_K - s, :]
        tail_ref[:, c0:c0 + tile] = x[n - SUBLANES:, :]
        y = _silu(conv)
        if c0 >= 2 * nqk:
            pb_ref[:, c0:c0 + tile] = y
        else:
            scale = B_DK ** -0.5 if c0 < nqk else 1.0
            for j in range(0, tile, B_DK):
                yh = y[:, j:j + B_DK]
                inv_norm = lax.rsqrt(jnp.sum(yh * yh, axis=-1, keepdims=True) + NORM_EPS)
                pb_ref[:, c0 + j:c0 + j + B_DK] = yh * (inv_norm * scale)

    def plain_tile(out_ref, c_out, c_w, act):
        y = project(c_w)
        out_ref[:, c_out:c_out + tile] = _silu(y) if act else y

    heavy += [functools.partial(conv_tile, c0) for c0 in range(0, ncv, tile)]
    light += [functools.partial(plain_tile, pb_ref, c0, off_b + c0, True) for c0 in range(ncv, B_WIDTH, tile)]

    nk = A_HEADS * A_DK
    nv = A_HEADS * A_DV

    def forget_tile(c0):
        z = project(nk + c0)
        lb = lb_ref[:, c0:c0 + tile]
        f = lb + (1.0 - lb) * _sigmoid(z)
        pa_ref[:, nk + c0:nk + c0 + tile] = (1.0 - lb) * _sigmoid(-z)
        for i, term in enumerate(_split3(jnp.log2(jnp.maximum(f, LOG_FLOOR)))):
            al_ref[:, i * nk + c0:i * nk + c0 + tile] = term

    heavy += [functools.partial(forget_tile, c0) for c0 in range(0, nk, tile)]
    light += [functools.partial(plain_tile, pa_ref, c0, c0, True) for c0 in range(0, nk, tile)]
    light += [functools.partial(plain_tile, pa_ref, c0, c0, False) for c0 in range(2 * nk, 2 * nk + nv, tile)]
    light += [functools.partial(plain_tile, pa_ref, c0, c0, True) for c0 in range(2 * nk + nv, A_WIDTH, tile)]

    nqc = C_HEADS * C_DK
    nvc = C_HEADS * C_DV
    first_half = (_iota2((n, LANES), 1) % C_DK) < (C_DK // 2)

    def rope_tile(c0):
        x = project(off_c + c0)
        cos = cos_ref[...]
        sin = sin_ref[...]
        scale = 1.0 if c0 < nqc else C_DK ** -0.5
        for j in range(0, tile, LANES):
            xj = x[:, j:j + LANES]
            rot = jnp.where(first_half, pltpu.roll(xj, LANES - C_DK // 2, 1), pltpu.roll(xj, C_DK // 2, 1))
            pc_ref[:, c0 + j:c0 + j + LANES] = (xj * cos + rot * sin) * scale

    light += [functools.partial(rope_tile, c0) for c0 in range(0, 2 * nqc, tile)]
    light += [functools.partial(plain_tile, pc_ref, c0, off_c + c0, False)
              for c0 in range(2 * nqc, 2 * nqc + nvc, tile)]
    light += [functools.partial(plain_tile, pc_ref, c0, off_c + c0, True)
              for c0 in range(2 * nqc + nvc, C_WIDTH, tile)]

    while heavy or light:
        if light:
            light.pop(0)()
        if heavy:
            heavy.pop(0)()


def _proj(x, layer, nw, w, cw, lb, alog, dtb, cos, sin, nblk):
    t, d = x.shape
    consts = (nw, w, cw, lb, alog, dtb)
    outs = ((A_WIDTH, F32), (3 * A_HEADS * A_DK, BF16), (B_WIDTH, F32), (G_WIDTH, F32), (C_WIDTH, F32))

    def rows(width):
        return pl.BlockSpec((ROW_BLOCK, width), lambda i: (i, 0))

    return pl.pallas_call(
        functools.partial(_proj_kernel, nblk=nblk),
        grid=(t // ROW_BLOCK,),
        in_specs=[rows(d)] + [_layer_spec(c.shape, layer) for c in consts] + [rows(LANES), rows(LANES)],
        out_specs=[rows(n) for n, _ in outs],
        out_shape=[jax.ShapeDtypeStruct((t, n), dt) for n, dt in outs],
        scratch_shapes=[pltpu.VMEM((SUBLANES, B_HEADS * (2 * B_DK + B_DV)), F32)],
        compiler_params=_params(("arbitrary",)),
        name="in_proj",
    )(x, *consts, cos, sin)


def _lower_bounds_kernel(p_ref, o_ref):
    p = p_ref[...]
    e = jnp.exp(p - jnp.max(p, axis=0, keepdims=True))
    s = e / jnp.sum(e, axis=0, keepdims=True)
    rows = [s[0:1]]
    for l in range(1, p.shape[0]):
        rows.append(rows[-1] + s[l:l + 1])
    o_ref[...] = jnp.concatenate(rows, axis=0) - s[0:1]


def _lower_bounds(p):
    return pl.pallas_call(_lower_bounds_kernel, out_shape=jax.ShapeDtypeStruct(p.shape, F32),
                          name="hgrn_lower_bounds")(p)


def _rope_kernel(pos_ref, inv_ref, cos_ref, sin_ref):
    ang = pos_ref[...].astype(F32) * inv_ref[...]
    cos_ref[...] = jnp.cos(ang)
    sin_ref[...] = jnp.sin(ang)


def _rope_tables(positions):
    t = positions.size
    half = C_DK // 2
    per_row = LANES // half
    inv = ROPE_BASE ** (-jnp.arange(half, dtype=F32) / half)
    inv = jnp.tile(inv, per_row)[None, :]
    pos = jnp.repeat(positions.reshape(t // per_row, per_row), half, axis=1)
    rows = t // per_row
    blk = pl.BlockSpec((ROW_BLOCK, LANES), lambda i: (i, 0))
    out = jax.ShapeDtypeStruct((rows, LANES), F32)
    cos, sin = pl.pallas_call(
        _rope_kernel,
        grid=(rows // ROW_BLOCK,),
        in_specs=[blk, _const_spec(inv.shape)],
        out_specs=[blk, blk],
        out_shape=[out, out],
        compiler_params=_params(("parallel",)),
        name="rope_tables",
    )(pos, inv)
    cos = cos.reshape(t, half)
    sin = sin.reshape(t, half)
    reps = LANES // C_DK
    return jnp.tile(cos, (1, 2 * reps)), jnp.tile(jnp.concatenate([-sin, sin], axis=1), (1, reps))


def _head_mean_square(o, width):
    n = o.shape[1]
    r = _iota2((n, n), 0) // width
    c = _iota2((n, n), 1) // width
    ones = jnp.where(r == c, 1.0 / width, 0.0).astype(BF16)
    sq = o * o
    hi = sq.astype(BF16)
    lo = (sq - hi.astype(F32)).astype(BF16)
    return (jnp.dot(hi, ones, preferred_element_type=F32)
            + jnp.dot(lo, ones, preferred_element_type=F32))


def _mixer_call(kernel, ins, in_specs, out_width, scratch, batch, seq, name):
    nblk = seq // MIX_BLOCK
    return pl.pallas_call(
        kernel,
        grid=(batch, nblk),
        in_specs=in_specs,
        out_specs=pl.BlockSpec((MIX_BLOCK, out_width), lambda b, s: (b * nblk + s, 0)),
        out_shape=jax.ShapeDtypeStruct((batch * seq, out_width), BF16),
        scratch_shapes=scratch,
        compiler_params=_params(("parallel", "arbitrary")),
        name=name,
    )(*ins)


def _seq_spec(width, nblk):
    return pl.BlockSpec((MIX_BLOCK, width), lambda b, s: (b * nblk + s, 0))


def _lead_spec(width, nblk):
    return pl.BlockSpec((MIX_BLOCK, width), lambda b, s: (b * nblk + jnp.minimum(s, nblk - 1), 0))


def _lag_spec(width, nblk, col_block=0):
    return pl.BlockSpec((MIX_BLOCK, width), lambda b, s: (b * nblk + jnp.maximum(s - 1, 0), col_block))


def _skewed_mixer_call(kernel, ins, in_specs, out_width, scratch, batch, seq, name):
    nblk = seq // MIX_BLOCK
    return pl.pallas_call(
        kernel,
        grid=(batch, nblk + 1),
        in_specs=in_specs,
        out_specs=_lag_spec(out_width, nblk),
        out_shape=jax.ShapeDtypeStruct((batch * seq, out_width), BF16),
        scratch_shapes=scratch,
        compiler_params=_params(("arbitrary", "arbitrary")),
        name=name,
    )(*ins)


HGRN_LEVELS = tuple(CHUNK >> (i + 1) for i in range(CHUNK.bit_length() - 1))


def _hgrn_decay_weights():
    n = CHUNK
    i = np.arange(n)[:, None]
    t = np.arange(n)[None, :]
    blocks = [t <= i, t > i]
    for h in HGRN_LEVELS:
        ref = (i // (2 * h)) * 2 * h + h - 1
        upper = (i % (2 * h)) >= h
        blocks.append(np.where(upper, (t > ref) & (t <= i), (t > i) & (t <= ref)))
    w = np.concatenate(blocks, axis=0).astype(np.float32)
    return jnp.asarray(np.concatenate([w, w, w], axis=1), dtype=BF16)


def _hgrn_kernel(pa_ref, al_ref, nw_ref, w_ref, o_ref, st_ref, raw_ref):
    @pl.when(pl.program_id(1) == 0)
    def _():
        st_ref[...] = jnp.zeros_like(st_ref)

    nk = A_HEADS * A_DK
    nv = A_HEADS * A_DV
    heads = range(A_HEADS)
    nchunks = MIX_BLOCK // CHUNK
    r2 = _iota2((CHUNK, CHUNK), 0)
    c2 = _iota2((CHUNK, CHUNK), 1)
    row = _iota2((CHUNK, LANES), 0)
    lane = _iota2((CHUNK, LANES), 1)
    uppers = [(row % (2 * h)) >= h for h in HGRN_LEVELS]
    masks = [((r2 // (2 * h)) == (c2 // (2 * h))) & ((r2 % (2 * h)) >= h) & ((c2 % (2 * h)) < h)
             for h in HGRN_LEVELS]

    attns, q_ins, k_sts, decays = [], [], [], []
    for ci in range(nchunks):
        r0 = ci * CHUNK
        q = pa_ref[r0:r0 + CHUNK, 0:nk]
        k = pa_ref[r0:r0 + CHUNK, nk:2 * nk]
        log_f = jnp.concatenate([al_ref[r0:r0 + CHUNK, i * nk:(i + 1) * nk] for i in range(3)], axis=0)
        e = jnp.exp2(_bdot(w_ref[...], log_f))
        q_ins.append((q * e[0:CHUNK]).astype(BF16))
        k_sts.append((k * e[CHUNK:2 * CHUNK]).astype(BF16))
        decays.append(e[CHUNK - 1:CHUNK])
        for h in heads:
            sl = slice(h * A_DK, (h + 1) * A_DK)
            q_h, k_h = q[:, sl], k[:, sl]
            attn = jnp.where(r2 == c2, jnp.sum(q_h * k_h, axis=-1, keepdims=True), 0.0)
            for lv in range(len(HGRN_LEVELS)):
                e_lv = e[(2 + lv) * CHUNK:(3 + lv) * CHUNK, sl]
                x = (jnp.where(uppers[lv], q_h, k_h) * e_lv).astype(BF16)
                p = lax.dot_general(x, x, (((1,), (1,)), ((), ())), preferred_element_type=F32)
                attn = attn + jnp.where(masks[lv], p, 0.0)
            attns.append(attn.astype(BF16))

    sts = [st_ref[pair] for pair in range(A_HEADS // 2)]
    for ci in range(nchunks):
        r0 = ci * CHUNK
        outs = []
        for pair in range(A_HEADS // 2):
            vp = pa_ref[r0:r0 + CHUNK, 2 * nk + pair * LANES:2 * nk + (pair + 1) * LANES]
            vpb = vp.astype(BF16)
            vpt = vp.T.astype(BF16)
            st = sts[pair]
            stb = st.astype(BF16)
            o_pair = jnp.zeros((CHUNK, LANES), F32)
            new_rows = []
            for sub in range(2):
                h = 2 * pair + sub
                sl = slice(h * A_DK, (h + 1) * A_DK)
                o_h = (lax.dot_general(q_ins[ci][:, sl], stb, (((1,), (1,)), ((), ())),
                                       preferred_element_type=F32)
                       + _bdot(attns[ci * A_HEADS + h], vpb))
                o_pair = jnp.where((lane // A_DV) == sub, o_h, o_pair)
                rows = slice(sub * A_DV, (sub + 1) * A_DV)
                new_rows.append(decays[ci][:, sl] * st[rows, :] + _bdot(vpt[rows, :], k_sts[ci][:, sl]))
            sts[pair] = jnp.concatenate(new_rows, axis=0)
            outs.append(o_pair)
        raw_ref[r0:r0 + CHUNK, :] = jnp.concatenate(outs, axis=1)
    for pair in range(A_HEADS // 2):
        st_ref[pair] = sts[pair]

    o = raw_ref[...]
    gate = pa_ref[:, 2 * nk + nv:2 * nk + 2 * nv]
    o = o * lax.rsqrt(_head_mean_square(o, A_DV) + NORM_EPS) * nw_ref[...] * gate
    o_ref[...] = o.astype(BF16)


def _hgrn(pa, al, layer, nw, batch, seq):
    nblk = seq // MIX_BLOCK
    nv = A_HEADS * A_DV
    w = _hgrn_decay_weights()
    return _mixer_call(
        _hgrn_kernel, (pa, al, nw, w),
        [_seq_spec(A_WIDTH, nblk), _seq_spec(al.shape[1], nblk), _layer_spec(nw.shape, layer),
         _const_spec(w.shape)],
        nv,
        [pltpu.VMEM((A_HEADS // 2, 2 * A_DV, A_DK), F32), pltpu.VMEM((MIX_BLOCK, nv), F32)],
        batch, seq, "hgrn2")


def _packed_operands(a):
    n, w = a.shape
    hi = a.astype(BF16)
    hi_f = hi.astype(F32)
    lo_f = a - hi_f
    lo = lo_f.astype(BF16)
    col = jnp.where(_iota2((n, w), 1) < w // 2, hi_f, lo_f).astype(BF16)
    return jnp.concatenate([col, col], axis=1), jnp.concatenate([hi, hi, lo, lo], axis=0)


def _packed_rhs(b):
    hi, lo = _split(b)
    return jnp.concatenate([hi, hi, lo, lo], axis=0)


def _unit_lower_inverses(ms, between):
    n = ms[0].shape[0]
    r = _iota2((n, 2 * n), 0)
    c = _iota2((n, 2 * n), 1) % n
    eye = jnp.where(r == c, 1.0, 0.0).astype(F32)
    inv_ops = None
    inner = 1
    while inner < n:
        outer = inner * 4
        mask = ((r // outer) == (c // outer)) & ((r // inner) != (c // inner))
        ps = [jnp.where(mask, m, 0.0) for m in ms]
        if inv_ops is not None:
            ps = [_bdot(inv[0], _packed_rhs(p)) for inv, p in zip(inv_ops, ps)]
            between()
        p_ops = [_packed_operands(p) for p in ps]
        pps = [_bdot(po[0], po[1]) for po in p_ops]
        between()
        corrs = [_bdot(_packed_operands(eye - p)[0], _packed_rhs(eye + pp)) for p, pp in zip(ps, pps)]
        between()
        if inv_ops is not None:
            corrs = [_bdot(_packed_operands(cr)[0], inv[1]) for cr, inv in zip(corrs, inv_ops)]
            between()
        inv_ops = [_packed_operands(cr) for cr in corrs]
        inner = outer
    return [inv[0] for inv in inv_ops]


def _gdn_kernel(pb_ref, pg_ref, gate_ref, nw_ref, o_ref, st_ref, uw_ref, attn_ref, qe_ref, kt_ref, dec_ref):
    carried = (uw_ref, attn_ref, qe_ref, kt_ref, dec_ref)

    @pl.when(pl.program_id(1) == 0)
    def _():
        for ref in (st_ref,) + carried:
            ref[...] = jnp.zeros_like(ref)

    nqk = B_HEADS * B_DK
    heads = range(B_HEADS)
    nchunks = MIX_BLOCK // CHUNK
    nw = nw_ref[...]
    rd = pl.program_id(1) % 2
    wr = 1 - rd

    carry = {"st": [st_ref[h] for h in heads]}

    def first_half(ci):
        idx = [ci * B_HEADS + h for h in heads]
        st_bs = [st.astype(BF16) for st in carry["st"]]
        carry["vn"] = [(uw_ref[rd, i, :, 0:B_DV]
                        - _bdot(uw_ref[rd, i, :, B_DV:].astype(BF16), sb)).astype(BF16)
                       for i, sb in zip(idx, st_bs)]
        carry["o"] = [_bdot(qe_ref[rd, i], sb) for i, sb in zip(idx, st_bs)]

    def second_half(ci):
        r0 = ci * CHUNK
        idx = [ci * B_HEADS + h for h in heads]
        outs = [o + _bdot(attn_ref[rd, i], vn) for i, o, vn in zip(idx, carry["o"], carry["vn"])]
        carry["st"] = [dec_ref[rd, i] * st + _bdot(kt_ref[rd, i], vn)
                       for i, st, vn in zip(idx, carry["st"], carry["vn"])]
        for h in heads:
            o = outs[h]
            gate = gate_ref[r0:r0 + CHUNK, h * B_DV:(h + 1) * B_DV]
            o = o * lax.rsqrt(jnp.mean(o * o, axis=-1, keepdims=True) + NORM_EPS) * nw * gate
            o_ref[r0:r0 + CHUNK, h * B_DV:(h + 1) * B_DV] = o.astype(BF16)

    steps = iter([functools.partial(f, ci) for ci in range(nchunks) for f in (first_half, second_half)])

    def between():
        step = next(steps, None)
        if step is not None:
            step()

    r2 = _iota2((CHUNK, CHUNK), 0)
    c2 = _iota2((CHUNK, CHUNK), 1)
    rdup = _iota2((CHUNK, 2 * CHUNK), 0)
    cdup = _iota2((CHUNK, 2 * CHUNK), 1) % CHUNK
    tril3 = jnp.where(_iota2((CHUNK, 3 * CHUNK), 0) >= _iota2((CHUNK, 3 * CHUNK), 1) % CHUNK,
                      1.0, 0.0).astype(BF16)
    ms, rhss = [], []
    for ci in range(nchunks):
        r0 = ci * CHUNK
        gates = pg_ref[r0:r0 + CHUNK, :]
        gcum = _bdot(tril3, jnp.concatenate(_split3(gates), axis=0))
        gcum_t = jnp.concatenate([gcum, gcum], axis=0).T

        for h in heads:
            q = pb_ref[r0:r0 + CHUNK, h * B_DK:(h + 1) * B_DK]
            k = pb_ref[r0:r0 + CHUNK, nqk + h * B_DK:nqk + (h + 1) * B_DK]
            v = pb_ref[r0:r0 + CHUNK, 2 * nqk + h * B_DV:2 * nqk + (h + 1) * B_DV]
            beta = jnp.broadcast_to(gates[:, h:h + 1], (CHUNK, LANES))
            gc = jnp.broadcast_to(gcum[:, B_HEADS + h:B_HEADS + h + 1], (CHUNK, LANES))
            gc_row = jnp.broadcast_to(gcum_t[B_HEADS + h:B_HEADS + h + 1, :], (CHUNK, 2 * CHUNK))
            gamma = jnp.exp(jnp.minimum(gc - gc_row, 0.0))
            kb = k.astype(BF16)
            kk = lax.dot_general(kb, jnp.concatenate([kb, kb], axis=0), (((1,), (1,)), ((), ())),
                                 preferred_element_type=F32)
            ms.append(jnp.where(rdup > cdup, beta * kk * gamma, 0.0))
            i = ci * B_HEADS + h
            attn_ref[wr, i] = jnp.where(r2 >= c2, _dot_nt(q, kb) * gamma[:, 0:CHUNK], 0.0).astype(BF16)
            e_gc = jnp.exp(gc)
            rhss.append(_packed_rhs(jnp.concatenate([v * beta, k * (beta * e_gc)], axis=1)))
            g_last = gc[CHUNK - 1:CHUNK, :]
            qe_ref[wr, i] = (q * e_gc).astype(BF16)
            kt_ref[wr, i] = (k * jnp.exp(g_last - gc)).T.astype(BF16)
            dec_ref[wr, i] = jnp.exp(g_last)

    t_ops = _unit_lower_inverses(ms, between)
    uws = [_bdot(t, rhs) for t, rhs in zip(t_ops, rhss)]
    for step in steps:
        step()
    for h in heads:
        st_ref[h] = carry["st"][h]
    for i in range(nchunks * B_HEADS):
        uw_ref[wr, i] = uws[i]


def _gdn(pb, pg, layer, nw, batch, seq):
    nblk = seq // MIX_BLOCK
    n = (MIX_BLOCK // CHUNK) * B_HEADS
    nz = B_HEADS * B_DV
    return _skewed_mixer_call(
        _gdn_kernel, (pb, pg, pb, nw),
        [_lead_spec(B_WIDTH, nblk), _lead_spec(G_WIDTH, nblk),
         _lag_spec(nz, nblk, (B_WIDTH - nz) // nz), _layer_spec(nw.shape, layer)],
        nz,
        [pltpu.VMEM((B_HEADS, B_DK, B_DV), F32),
         pltpu.VMEM((2, n, CHUNK, 2 * B_DV), F32), pltpu.VMEM((2, n, CHUNK, CHUNK), BF16),
         pltpu.VMEM((2, n, CHUNK, B_DK), BF16), pltpu.VMEM((2, n, B_DK, CHUNK), BF16),
         pltpu.VMEM((2, n, 1, B_DV), F32)],
        batch, seq, "gated_deltanet")


def _ret_kernel(pc_ref, dmat_ref, qdec_ref, kdec_ref, sdec_ref, o_ref, st_ref):
    @pl.when(pl.program_id(1) == 0)
    def _():
        st_ref[...] = jnp.zeros_like(st_ref)

    nqk = C_HEADS * C_DK
    nv = C_HEADS * C_DV
    n = MIX_BLOCK
    lane = _iota2((n, LANES), 1)
    r2 = _iota2((LANES, LANES), 0) // C_DK
    c2 = _iota2((LANES, LANES), 1) // C_DV

    outs = []
    for pair in range(C_HEADS // 2):
        sl = slice(pair * LANES, (pair + 1) * LANES)
        q = pc_ref[:, sl]
        k = pc_ref[:, nqk + pair * LANES:nqk + (pair + 1) * LANES]
        v = pc_ref[:, 2 * nqk + pair * LANES:2 * nqk + (pair + 1) * LANES]
        st = st_ref[pair]
        o_pair = _dot(q * qdec_ref[:, sl], st)
        for sub in range(2):
            h = 2 * pair + sub
            k_h = jnp.where((lane // C_DK) == sub, k, 0.0)
            scores = _dot_nt(q, k_h) * dmat_ref[h]
            o_h = _dot(scores, v)
            o_pair = o_pair + jnp.where((lane // C_DV) == sub, o_h, 0.0)
        ks = k * kdec_ref[:, sl]
        st_ref[pair] = sdec_ref[pair] * st + jnp.where(r2 == c2, _dot(ks.T, v), 0.0)
        outs.append(o_pair)
    o = jnp.concatenate(outs, axis=1)
    gate = pc_ref[:, 2 * nqk + nv:2 * nqk + 2 * nv]
    o = o * lax.rsqrt(_head_mean_square(o, C_DV) + NORM_EPS) * gate
    o_ref[...] = o.astype(BF16)


def _retention_tables():
    n = MIX_BLOCK
    log_gamma = jnp.log1p(-jnp.exp2(-5.0 - jnp.arange(C_HEADS, dtype=F32)))
    idx = jnp.arange(n, dtype=F32)
    rel = idx[:, None] - idx[None, :]
    dmat = jnp.where(rel >= 0, jnp.exp(jnp.where(rel >= 0, log_gamma[:, None, None] * rel, 0.0)), 0.0)
    lane_gamma = jnp.repeat(log_gamma, C_DK)[None, :]
    qdec = jnp.exp(lane_gamma * (idx[:, None] + 1.0))
    kdec = jnp.exp(lane_gamma * (n - 1.0 - idx[:, None]))
    sdec = jnp.exp(lane_gamma * float(n)).reshape(C_HEADS // 2, LANES, 1)
    sdec = jnp.broadcast_to(sdec, (C_HEADS // 2, LANES, LANES))
    return dmat, qdec, kdec, sdec


def _retention(pc, tables, batch, seq):
    nblk = seq // MIX_BLOCK
    return _mixer_call(
        _ret_kernel, (pc,) + tuple(tables),
        [_seq_spec(C_WIDTH, nblk)] + [_const_spec(t.shape) for t in tables],
        C_HEADS * C_DV,
        [pltpu.VMEM((C_HEADS // 2, LANES, LANES), F32)],
        batch, seq, "retention")


def _permute_w_in(w_in):
    na, nb = A_WIDTH, B_WIDTH
    gates = w_in[..., na + nb:na + nb + 2 * B_HEADS]
    rest = w_in[..., na + nb + 2 * B_HEADS:]
    pad = jnp.zeros(w_in.shape[:-1] + (G_WIDTH - 2 * B_HEADS,), w_in.dtype)
    return jnp.concatenate([w_in[..., :na + nb], rest, gates, pad], axis=-1)


def _lane_row(vals, offset):
    row = jnp.zeros((vals.shape[0], LANES), F32)
    return row.at[:, offset:offset + vals.shape[1]].set(vals.astype(F32))[:, None, :]


def kernel(x, positions, ffn1_norm, ffn1_w_gate, ffn1_w_up, ffn1_w_down, mix_norm, w_in,
           hgrn_lower_bounds, hgrn_norm, gdn_conv, gdn_a_log, gdn_dt_bias, gdn_norm, w_out,
           ffn2_norm, ffn2_w_gate, ffn2_w_up, ffn2_w_down, final_norm):
    batch, seq, d = x.shape
    depth = w_in.shape[0]
    t = batch * seq
    assert seq % MIX_BLOCK == 0 and seq % ROW_BLOCK == 0 and t % (ROW_BLOCK * LANES // (C_DK // 2)) == 0

    bf = lambda w: w.astype(BF16)
    row = lambda w: w.astype(F32)[:, None, :]
    w_in_p = bf(_permute_w_in(w_in))
    wg1, wu1, wd1 = bf(ffn1_w_gate), bf(ffn1_w_up), bf(ffn1_w_down)
    wg2, wu2, wd2 = bf(ffn2_w_gate), bf(ffn2_w_up), bf(ffn2_w_down)
    w_out_b = bf(w_out)
    n1, m1, n2 = row(ffn1_norm), row(mix_norm), row(ffn2_norm)
    hgrn_nw = row(jnp.tile(hgrn_norm, (1, A_HEADS)))
    gdn_nw = row(gdn_norm)
    alog = _lane_row(gdn_a_log, B_HEADS)
    dtb = _lane_row(gdn_dt_bias, B_HEADS)
    conv_w = gdn_conv.astype(F32)
    fw = final_norm.astype(F32)[None, :]

    lbs = _lower_bounds(hgrn_lower_bounds.astype(F32))[:, None, :]
    cos, sin = _rope_tables(positions)
    tables = _retention_tables()

    xt = x.reshape(t, d).astype(F32)
    for l in range(depth):
        xt = _ffn1(xt, l, n1, wg1, wu1, wd1)
        pa, al, pb, pg, pc = _proj(xt, l, m1, w_in_p, conv_w, lbs, alog, dtb, cos, sin, seq // ROW_BLOCK)
        oa = _hgrn(pa, al, l, hgrn_nw, batch, seq)
        ob = _gdn(pb, pg, l, gdn_nw, batch, seq)
        oc = _retention(pc, tables, batch, seq)
        xt = _ffn2(xt, oa, ob, oc, l, w_out_b, n2, wg2, wu2, wd2, fw, l == depth - 1)
    return xt.reshape(batch, seq, d).astype(x.dtype)
```

```python
import functools
import itertools

import jax
import jax.numpy as jnp
import numpy as np
from jax import lax
from jax.experimental import pallas as pl
from jax.experimental.pallas import tpu as pltpu

F32 = jnp.float32
BF16 = jnp.bfloat16

NORM_EPS = 1e-6
LOG_FLOOR = 1e-20
ROPE_BASE = 10000.0

A_HEADS, A_DK, A_DV = 4, 128, 64
B_HEADS, B_DK, B_DV = 4, 128, 128
C_HEADS, C_DK, C_DV = 4, 64, 64
CONV_K = 4
LANES = 128
SUBLANES = 8

CHUNK = 64
MIX_BLOCK = 256
ROW_BLOCK = 1024
FFN_SUB_BLOCK = 256
PROJ_BLOCK = 256
PROJ_TILE = 256
VMEM_LIMIT = 56 * 1024 * 1024

A_WIDTH = 2 * A_HEADS * A_DK + 2 * A_HEADS * A_DV
B_WIDTH = 2 * B_HEADS * B_DK + 2 * B_HEADS * B_DV
C_WIDTH = 2 * C_HEADS * C_DK + 2 * C_HEADS * C_DV
G_WIDTH = LANES


def _dot(a, b):
    return jnp.dot(a.astype(BF16), b.astype(BF16), preferred_element_type=F32)


def _dot_nt(a, b):
    return lax.dot_general(a.astype(BF16), b.astype(BF16), (((1,), (1,)), ((), ())),
                           preferred_element_type=F32)


def _bdot(a, b):
    return jnp.dot(a, b, preferred_element_type=F32)


def _iota2(shape, dim):
    return lax.broadcasted_iota(jnp.int32, shape, dim)


def _softplus(x):
    return jnp.maximum(x, 0.0) + jnp.log1p(jnp.exp(-jnp.abs(x)))


def _split(a):
    hi = a.astype(BF16)
    lo = (a - hi.astype(F32)).astype(BF16)
    return hi, lo


def _split3(a):
    t1 = a.astype(BF16)
    rem = a - t1.astype(F32)
    t2 = rem.astype(BF16)
    return t1, t2, (rem - t2.astype(F32)).astype(BF16)


def _sigmoid(x):
    return jax.nn.sigmoid(x)


def _silu(x):
    return x * jax.nn.sigmoid(x)


def _rms(x, w):
    return x * lax.rsqrt(jnp.mean(x * x, axis=-1, keepdims=True) + NORM_EPS) * w


def _const_spec(shape):
    nd = len(shape)
    return pl.BlockSpec(shape, lambda *_: (0,) * nd, pipeline_mode=pl.Buffered(1))


def _layer_spec(shape, layer):
    return pl.BlockSpec((None,) + tuple(shape[1:]), lambda *_: (layer, 0, 0), pipeline_mode=pl.Buffered(1))


def _params(semantics):
    return pltpu.CompilerParams(dimension_semantics=semantics, vmem_limit_bytes=VMEM_LIMIT)


def _ffn_rows(x, nw_ref, wg_ref, wu_ref, wd_ref):
    h = _rms(x, nw_ref[...]).astype(BF16)
    g = jnp.dot(h, wg_ref[...], preferred_element_type=F32)
    u = jnp.dot(h, wu_ref[...], preferred_element_type=F32)
    a = (_silu(g) * u).astype(BF16)
    return x + 0.5 * jnp.dot(a, wd_ref[...], preferred_element_type=F32)


def _sub_blocks():
    return [slice(r, r + FFN_SUB_BLOCK) for r in range(0, ROW_BLOCK, FFN_SUB_BLOCK)]


def _ffn1_kernel(x_ref, nw_ref, wg_ref, wu_ref, wd_ref, o_ref):
    for rows in _sub_blocks():
        o_ref[rows, :] = _ffn_rows(x_ref[rows, :], nw_ref, wg_ref, wu_ref, wd_ref)


def _ffn2_kernel(x_ref, oa_ref, ob_ref, oc_ref, wo_ref, nw_ref, wg_ref, wu_ref, wd_ref, fw_ref, o_ref,
                 *, final):
    na, nb = oa_ref.shape[1], ob_ref.shape[1]
    for rows in _sub_blocks():
        x = x_ref[rows, :]
        x = x + jnp.dot(oa_ref[rows, :], wo_ref[0:na, :], preferred_element_type=F32)
        x = x + jnp.dot(ob_ref[rows, :], wo_ref[na:na + nb, :], preferred_element_type=F32)
        x = x + jnp.dot(oc_ref[rows, :], wo_ref[na + nb:, :], preferred_element_type=F32)
        y = _ffn_rows(x, nw_ref, wg_ref, wu_ref, wd_ref)
        if final:
            y = _rms(y, fw_ref[...])
        o_ref[rows, :] = y


def _ffn1(x, layer, nw, wg, wu, wd):
    t, d = x.shape
    row = pl.BlockSpec((ROW_BLOCK, d), lambda i: (i, 0))
    return pl.pallas_call(
        _ffn1_kernel,
        grid=(t // ROW_BLOCK,),
        in_specs=[row] + [_layer_spec(c.shape, layer) for c in (nw, wg, wu, wd)],
        out_specs=row,
        out_shape=jax.ShapeDtypeStruct((t, d), F32),
        compiler_params=_params(("parallel",)),
        name="ffn1",
    )(x, nw, wg, wu, wd)


def _ffn2(x, oa, ob, oc, layer, wo, nw, wg, wu, wd, fw, final):
    t, d = x.shape

    def rows(a):
        return pl.BlockSpec((ROW_BLOCK, a.shape[1]), lambda i: (i, 0))

    consts = (wo, nw, wg, wu, wd)
    return pl.pallas_call(
        functools.partial(_ffn2_kernel, final=final),
        grid=(t // ROW_BLOCK,),
        in_specs=([rows(x), rows(oa), rows(ob), rows(oc)] + [_layer_spec(c.shape, layer) for c in consts]
                  + [_const_spec(fw.shape)]),
        out_specs=rows(x),
        out_shape=jax.ShapeDtypeStruct((t, d), F32),
        compiler_params=_params(("parallel",)),
        name="ffn2",
    )(x, oa, ob, oc, *consts, fw)


def _proj_kernel(x_ref, nw_ref, w_ref, cw_ref, lb_ref, alog_ref, dtb_ref, cos_ref, sin_ref,
                 pa_ref, al_ref, pb_ref, pg_ref, pc_ref, tail_ref, *, nblk):
    @pl.when(pl.program_id(0) % nblk == 0)
    def _():
        tail_ref[...] = jnp.zeros_like(tail_ref)

    n = PROJ_BLOCK
    h = _rms(x_ref[...], nw_ref[...]).astype(BF16)
    off_b = A_WIDTH
    off_c = A_WIDTH + B_WIDTH
    off_g = A_WIDTH + B_WIDTH + C_WIDTH
    tile = PROJ_TILE

    def project(c0, width=tile):
        return _bdot(h, w_ref[:, c0:c0 + width])

    heavy, light = [], []

    def gates():
        pg = project(off_g, G_WIDTH)
        log_decay = -jnp.exp(alog_ref[...]) * _softplus(pg + dtb_ref[...])
        pg_ref[...] = jnp.where(_iota2((n, G_WIDTH), 1) < B_HEADS, _sigmoid(pg), log_decay)

    light.append(gates)

    nqk = B_HEADS * B_DK
    ncv = 2 * nqk + B_HEADS * B_DV
    row8 = _iota2((SUBLANES, tile), 0)

    def conv_tile(c0):
        x = project(off_b + c0)
        cw = cw_ref[:, c0:c0 + tile]
        prev = tail_ref[:, c0:c0 + tile]
        conv = x * cw[CONV_K - 1:CONV_K, :]
        for s in range(1, CONV_K):
            xs = pltpu.roll(x, s, 0)
            top = jnp.where(row8 < s, pltpu.roll(prev, s, 0), xs[0:SUBLANES, :])
            xs = jnp.concatenate([top, xs[SUBLANES:, :]], axis=0)
            conv = conv + xs * cw[CONV_K - 1 - s:CONV_K - s, :]
        tail_ref[:, c0:c0 + tile] = x[n - SUBLANES:, :]
        y = _silu(conv)
        if c0 >= 2 * nqk:
            pb_ref[:, c0:c0 + tile] = y
        else:
            scale = B_DK ** -0.5 if c0 < nqk else 1.0
            for j in range(0, tile, B_DK):
                yh = y[:, j:j + B_DK]
                inv_norm = lax.rsqrt(jnp.sum(yh * yh, axis=-1, keepdims=True) + NORM_EPS)
                pb_ref[:, c0 + j:c0 + j + B_DK] = yh * (inv_norm * scale)

    def plain_tile(out_ref, c_out, c_w, act):
        y = project(c_w)
        out_ref[:, c_out:c_out + tile] = _silu(y) if act else y

    heavy += [functools.partial(conv_tile, c0) for c0 in range(0, ncv, tile)]
    light += [functools.partial(plain_tile, pb_ref, c0, off_b + c0, True) for c0 in range(ncv, B_WIDTH, tile)]

    nk = A_HEADS * A_DK
    nv = A_HEADS * A_DV

    def forget_tile(c0):
        z = project(nk + c0)
        lb = lb_ref[:, c0:c0 + tile]
        f = lb + (1.0 - lb) * _sigmoid(z)
        pa_ref[:, nk + c0:nk + c0 + tile] = (1.0 - lb) * _sigmoid(-z)
        for i, term in enumerate(_split3(jnp.log2(jnp.maximum(f, LOG_FLOOR)))):
            al_ref[:, i * nk + c0:i * nk + c0 + tile] = term

    heavy += [functools.partial(forget_tile, c0) for c0 in range(0, nk, tile)]
    light += [functools.partial(plain_tile, pa_ref, c0, c0, True) for c0 in range(0, nk, tile)]
    light += [functools.partial(plain_tile, pa_ref, c0, c0, False) for c0 in range(2 * nk, 2 * nk + nv, tile)]
    light += [functools.partial(plain_tile, pa_ref, c0, c0, True) for c0 in range(2 * nk + nv, A_WIDTH, tile)]

    nqc = C_HEADS * C_DK
    nvc = C_HEADS * C_DV
    first_half = (_iota2((n, LANES), 1) % C_DK) < (C_DK // 2)

    def rope_tile(c0):
        x = project(off_c + c0)
        cos = cos_ref[...]
        sin = sin_ref[...]
        scale = 1.0 if c0 < nqc else C_DK ** -0.5
        for j in range(0, tile, LANES):
            xj = x[:, j:j + LANES]
            rot = jnp.where(first_half, pltpu.roll(xj, LANES - C_DK // 2, 1), pltpu.roll(xj, C_DK // 2, 1))
            pc_ref[:, c0 + j:c0 + j + LANES] = (xj * cos + rot * sin) * scale

    light += [functools.partial(rope_tile, c0) for c0 in range(0, 2 * nqc, tile)]
    light += [functools.partial(plain_tile, pc_ref, c0, off_c + c0, False)
              for c0 in range(2 * nqc, 2 * nqc + nvc, tile)]
    light += [functools.partial(plain_tile, pc_ref, c0, off_c + c0, True)
              for c0 in range(2 * nqc + nvc, C_WIDTH, tile)]

    while heavy or light:
        if light:
            light.pop(0)()
        if heavy:
            heavy.pop(0)()


def _proj(x, layer, nw, w, cw, lb, alog, dtb, cos, sin, nblk):
    t, d = x.shape
    consts = (nw, w, cw, lb, alog, dtb)
    outs = ((A_WIDTH, F32), (3 * A_HEADS * A_DK, BF16), (B_WIDTH, F32), (G_WIDTH, F32), (C_WIDTH, F32))

    def rows(width):
        return pl.BlockSpec((PROJ_BLOCK, width), lambda i: (i, 0))

    return pl.pallas_call(
        functools.partial(_proj_kernel, nblk=nblk),
        grid=(t // PROJ_BLOCK,),
        in_specs=[rows(d)] + [_layer_spec(c.shape, layer) for c in consts] + [rows(LANES), rows(LANES)],
        out_specs=[rows(n) for n, _ in outs],
        out_shape=[jax.ShapeDtypeStruct((t, n), dt) for n, dt in outs],
        scratch_shapes=[pltpu.VMEM((SUBLANES, B_HEADS * (2 * B_DK + B_DV)), F32)],
        compiler_params=_params(("arbitrary",)),
        name="in_proj",
    )(x, *consts, cos, sin)


def _lower_bounds_kernel(p_ref, o_ref):
    p = p_ref[...]
    e = jnp.exp(p - jnp.max(p, axis=0, keepdims=True))
    s = e / jnp.sum(e, axis=0, keepdims=True)
    rows = [s[0:1]]
    for l in range(1, p.shape[0]):
        rows.append(rows[-1] + s[l:l + 1])
    o_ref[...] = jnp.concatenate(rows, axis=0) - s[0:1]


def _lower_bounds(p):
    return pl.pallas_call(_lower_bounds_kernel, out_shape=jax.ShapeDtypeStruct(p.shape, F32),
                          name="hgrn_lower_bounds")(p)


def _rope_kernel(pos_ref, inv_ref, cos_ref, sin_ref):
    ang = pos_ref[...].astype(F32) * inv_ref[...]
    cos_ref[...] = jnp.cos(ang)
    sin_ref[...] = jnp.sin(ang)


def _rope_tables(positions):
    t = positions.size
    half = C_DK // 2
    per_row = LANES // half
    inv = ROPE_BASE ** (-jnp.arange(half, dtype=F32) / half)
    inv = jnp.tile(inv, per_row)[None, :]
    pos = jnp.repeat(positions.reshape(t // per_row, per_row), half, axis=1)
    rows = t // per_row
    blk = pl.BlockSpec((PROJ_BLOCK, LANES), lambda i: (i, 0))
    out = jax.ShapeDtypeStruct((rows, LANES), F32)
    cos, sin = pl.pallas_call(
        _rope_kernel,
        grid=(rows // PROJ_BLOCK,),
        in_specs=[blk, _const_spec(inv.shape)],
        out_specs=[blk, blk],
        out_shape=[out, out],
        compiler_params=_params(("parallel",)),
        name="rope_tables",
    )(pos, inv)
    cos = cos.reshape(t, half)
    sin = sin.reshape(t, half)
    reps = LANES // C_DK
    return jnp.tile(cos, (1, 2 * reps)), jnp.tile(jnp.concatenate([-sin, sin], axis=1), (1, reps))


def _head_mean_square(o, width):
    n = o.shape[1]
    r = _iota2((n, n), 0) // width
    c = _iota2((n, n), 1) // width
    ones = jnp.where(r == c, 1.0 / width, 0.0).astype(BF16)
    sq = o * o
    hi = sq.astype(BF16)
    lo = (sq - hi.astype(F32)).astype(BF16)
    return (jnp.dot(hi, ones, preferred_element_type=F32)
            + jnp.dot(lo, ones, preferred_element_type=F32))


def _lead_spec(width, nblk):
    return pl.BlockSpec((MIX_BLOCK, width), lambda b, s: (b * nblk + jnp.minimum(s, nblk - 1), 0))


def _lag_spec(width, nblk, col_block=0):
    return pl.BlockSpec((MIX_BLOCK, width), lambda b, s: (b * nblk + jnp.maximum(s - 1, 0), col_block))


HGRN_LEVELS = tuple(CHUNK >> (i + 1) for i in range(CHUNK.bit_length() - 1))


def _hgrn_decay_weights():
    n = CHUNK
    i = np.arange(n)[:, None]
    t = np.arange(n)[None, :]
    blocks = [t <= i, t > i]
    for h in HGRN_LEVELS:
        ref = (i // (2 * h)) * 2 * h + h - 1
        upper = (i % (2 * h)) >= h
        blocks.append(np.where(upper, (t > ref) & (t <= i), (t > i) & (t <= ref)))
    w = np.concatenate(blocks, axis=0).astype(np.float32)
    return jnp.asarray(np.concatenate([w, w, w], axis=1), dtype=BF16)


def _hgrn_tasks(pa_ref, al_ref, nw_ref, w_ref, o_ref, st_ref, prev_ref, raw_ref, last):
    nk = A_HEADS * A_DK
    nv = A_HEADS * A_DV
    heads = range(A_HEADS)
    nchunks = MIX_BLOCK // CHUNK
    r2 = _iota2((CHUNK, CHUNK), 0)
    c2 = _iota2((CHUNK, CHUNK), 1)
    row = _iota2((CHUNK, LANES), 0)
    lane = _iota2((CHUNK, LANES), 1)
    uppers = [(row % (2 * h)) >= h for h in HGRN_LEVELS]
    masks = [((r2 // (2 * h)) == (c2 // (2 * h))) & ((r2 % (2 * h)) >= h) & ((c2 % (2 * h)) < h)
             for h in HGRN_LEVELS]
    sts = [jnp.where(last, prev_ref[pair], st_ref[pair]) for pair in range(A_HEADS // 2)]
    for pair in range(A_HEADS // 2):
        prev_ref[pair] = sts[pair]

    for ci in range(nchunks):
        r0 = ci * CHUNK
        q = pa_ref[r0:r0 + CHUNK, 0:nk]
        k = pa_ref[r0:r0 + CHUNK, nk:2 * nk]
        log_f = jnp.concatenate([al_ref[r0:r0 + CHUNK, i * nk:(i + 1) * nk] for i in range(3)], axis=0)
        e = jnp.exp2(_bdot(w_ref[...], log_f))
        q_in = (q * e[0:CHUNK]).astype(BF16)
        k_st = (k * e[CHUNK:2 * CHUNK]).astype(BF16)
        decay = e[CHUNK - 1:CHUNK]
        yield
        attns = []
        for h in heads:
            sl = slice(h * A_DK, (h + 1) * A_DK)
            q_h, k_h = q[:, sl], k[:, sl]
            attn = jnp.where(r2 == c2, jnp.sum(q_h * k_h, axis=-1, keepdims=True), 0.0)
            for lv in range(len(HGRN_LEVELS)):
                e_lv = e[(2 + lv) * CHUNK:(3 + lv) * CHUNK, sl]
                x = (jnp.where(uppers[lv], q_h, k_h) * e_lv).astype(BF16)
                p = lax.dot_general(x, x, (((1,), (1,)), ((), ())), preferred_element_type=F32)
                attn = attn + jnp.where(masks[lv], p, 0.0)
            attns.append(attn.astype(BF16))
            yield
        outs = []
        for pair in range(A_HEADS // 2):
            vp = pa_ref[r0:r0 + CHUNK, 2 * nk + pair * LANES:2 * nk + (pair + 1) * LANES]
            vpb = vp.astype(BF16)
            vpt = vp.T.astype(BF16)
            st = sts[pair]
            stb = st.astype(BF16)
            o_pair = jnp.zeros((CHUNK, LANES), F32)
            new_rows = []
            for sub in range(2):
                h = 2 * pair + sub
                sl = slice(h * A_DK, (h + 1) * A_DK)
                o_h = (lax.dot_general(q_in[:, sl], stb, (((1,), (1,)), ((), ())),
                                       preferred_element_type=F32)
                       + _bdot(attns[h], vpb))
                o_pair = jnp.where((lane // A_DV) == sub, o_h, o_pair)
                rows = slice(sub * A_DV, (sub + 1) * A_DV)
                new_rows.append(decay[:, sl] * st[rows, :] + _bdot(vpt[rows, :], k_st[:, sl]))
            sts[pair] = jnp.concatenate(new_rows, axis=0)
            outs.append(o_pair)
        raw_ref[r0:r0 + CHUNK, :] = jnp.concatenate(outs, axis=1)
        yield
    for pair in range(A_HEADS // 2):
        st_ref[pair] = sts[pair]
    o = raw_ref[...]
    gate = pa_ref[:, 2 * nk + nv:2 * nk + 2 * nv]
    o = o * lax.rsqrt(_head_mean_square(o, A_DV) + NORM_EPS) * nw_ref[...] * gate
    o_ref[...] = o.astype(BF16)
    yield


def _packed_operands(a):
    n, w = a.shape
    hi = a.astype(BF16)
    hi_f = hi.astype(F32)
    lo_f = a - hi_f
    lo = lo_f.astype(BF16)
    col = jnp.where(_iota2((n, w), 1) < w // 2, hi_f, lo_f).astype(BF16)
    return jnp.concatenate([col, col], axis=1), jnp.concatenate([hi, hi, lo, lo], axis=0)


def _packed_rhs(b):
    hi, lo = _split(b)
    return jnp.concatenate([hi, hi, lo, lo], axis=0)


def _unit_lower_inverses(ms, between):
    n = ms[0].shape[0]
    r = _iota2((n, 2 * n), 0)
    c = _iota2((n, 2 * n), 1) % n
    eye = jnp.where(r == c, 1.0, 0.0).astype(F32)
    inv_ops = None
    inner = 1
    while inner < n:
        outer = inner * 4
        mask = ((r // outer) == (c // outer)) & ((r // inner) != (c // inner))
        ps = [jnp.where(mask, m, 0.0) for m in ms]
        if inv_ops is not None:
            ps = [_bdot(inv[0], _packed_rhs(p)) for inv, p in zip(inv_ops, ps)]
            between()
        p_ops = [_packed_operands(p) for p in ps]
        pps = [_bdot(po[0], po[1]) for po in p_ops]
        between()
        corrs = [_bdot(_packed_operands(eye - p)[0], _packed_rhs(eye + pp)) for p, pp in zip(ps, pps)]
        between()
        if inv_ops is not None:
            corrs = [_bdot(_packed_operands(cr)[0], inv[1]) for cr, inv in zip(corrs, inv_ops)]
            between()
        inv_ops = [_packed_operands(cr) for cr in corrs]
        inner = outer
    return [inv[0] for inv in inv_ops]


def _gdn_body(pb_ref, pg_ref, gate_ref, nw_ref, o_ref, st_ref, uw_ref, attn_ref, qe_ref, kt_ref, dec_ref,
              other_work):
    nqk = B_HEADS * B_DK
    heads = range(B_HEADS)
    nchunks = MIX_BLOCK // CHUNK
    nw = nw_ref[...]
    rd = pl.program_id(1) % 2
    wr = 1 - rd

    carry = {"st": [st_ref[h] for h in heads]}

    def first_half(ci):
        idx = [ci * B_HEADS + h for h in heads]
        st_bs = [st.astype(BF16) for st in carry["st"]]
        carry["vn"] = [(uw_ref[rd, i, :, 0:B_DV]
                        - _bdot(uw_ref[rd, i, :, B_DV:].astype(BF16), sb)).astype(BF16)
                       for i, sb in zip(idx, st_bs)]
        carry["o"] = [_bdot(qe_ref[rd, i], sb) for i, sb in zip(idx, st_bs)]

    def second_half(ci):
        r0 = ci * CHUNK
        idx = [ci * B_HEADS + h for h in heads]
        outs = [o + _bdot(attn_ref[rd, i], vn) for i, o, vn in zip(idx, carry["o"], carry["vn"])]
        carry["st"] = [dec_ref[rd, i] * st + _bdot(kt_ref[rd, i], vn)
                       for i, st, vn in zip(idx, carry["st"], carry["vn"])]
        for h in heads:
            o = outs[h]
            gate = gate_ref[r0:r0 + CHUNK, h * B_DV:(h + 1) * B_DV]
            o = o * lax.rsqrt(jnp.mean(o * o, axis=-1, keepdims=True) + NORM_EPS) * nw * gate
            o_ref[r0:r0 + CHUNK, h * B_DV:(h + 1) * B_DV] = o.astype(BF16)

    steps = iter([functools.partial(f, ci) for ci in range(nchunks) for f in (first_half, second_half)])

    def between():
        step = next(steps, None)
        if step is not None:
            step()
        other_work()

    r2 = _iota2((CHUNK, CHUNK), 0)
    c2 = _iota2((CHUNK, CHUNK), 1)
    rdup = _iota2((CHUNK, 2 * CHUNK), 0)
    cdup = _iota2((CHUNK, 2 * CHUNK), 1) % CHUNK
    tril3 = jnp.where(_iota2((CHUNK, 3 * CHUNK), 0) >= _iota2((CHUNK, 3 * CHUNK), 1) % CHUNK,
                      1.0, 0.0).astype(BF16)
    ms, rhss = [], []
    for ci in range(nchunks):
        r0 = ci * CHUNK
        gates = pg_ref[r0:r0 + CHUNK, :]
        gcum = _bdot(tril3, jnp.concatenate(_split3(gates), axis=0))
        gcum_t = jnp.concatenate([gcum, gcum], axis=0).T

        for h in heads:
            q = pb_ref[r0:r0 + CHUNK, h * B_DK:(h + 1) * B_DK]
            k = pb_ref[r0:r0 + CHUNK, nqk + h * B_DK:nqk + (h + 1) * B_DK]
            v = pb_ref[r0:r0 + CHUNK, 2 * nqk + h * B_DV:2 * nqk + (h + 1) * B_DV]
            beta = jnp.broadcast_to(gates[:, h:h + 1], (CHUNK, LANES))
            gc = jnp.broadcast_to(gcum[:, B_HEADS + h:B_HEADS + h + 1], (CHUNK, LANES))
            gc_row = jnp.broadcast_to(gcum_t[B_HEADS + h:B_HEADS + h + 1, :], (CHUNK, 2 * CHUNK))
            gamma = jnp.exp(jnp.minimum(gc - gc_row, 0.0))
            kb = k.astype(BF16)
            kk = lax.dot_general(kb, jnp.concatenate([kb, kb], axis=0), (((1,), (1,)), ((), ())),
                                 preferred_element_type=F32)
            ms.append(jnp.where(rdup > cdup, beta * kk * gamma, 0.0))
            i = ci * B_HEADS + h
            attn_ref[wr, i] = jnp.where(r2 >= c2, _dot_nt(q, kb) * gamma[:, 0:CHUNK], 0.0).astype(BF16)
            e_gc = jnp.exp(gc)
            rhss.append(_packed_rhs(jnp.concatenate([v * beta, k * (beta * e_gc)], axis=1)))
            g_last = gc[CHUNK - 1:CHUNK, :]
            qe_ref[wr, i] = (q * e_gc).astype(BF16)
            kt_ref[wr, i] = (k * jnp.exp(g_last - gc)).T.astype(BF16)
            dec_ref[wr, i] = jnp.exp(g_last)
            other_work()

    t_ops = _unit_lower_inverses(ms, between)
    uws = [_bdot(t, rhs) for t, rhs in zip(t_ops, rhss)]
    other_work()
    for step in steps:
        step()
    for h in heads:
        st_ref[h] = carry["st"][h]
    for i in range(nchunks * B_HEADS):
        uw_ref[wr, i] = uws[i]


def _ret_tasks(pc_ref, dmat_ref, qdec_ref, kdec_ref, sdec_ref, o_ref, st_ref, prev_ref, last):
    nqk = C_HEADS * C_DK
    nv = C_HEADS * C_DV
    n = MIX_BLOCK
    lane = _iota2((n, LANES), 1)
    r2 = _iota2((LANES, LANES), 0) // C_DK
    c2 = _iota2((LANES, LANES), 1) // C_DV

    outs = []
    for pair in range(C_HEADS // 2):
        sl = slice(pair * LANES, (pair + 1) * LANES)
        q = pc_ref[:, sl]
        k = pc_ref[:, nqk + pair * LANES:nqk + (pair + 1) * LANES]
        v = pc_ref[:, 2 * nqk + pair * LANES:2 * nqk + (pair + 1) * LANES]
        st = jnp.where(last, prev_ref[pair], st_ref[pair])
        prev_ref[pair] = st
        o_pair = _dot(q * qdec_ref[:, sl], st)
        for sub in range(2):
            h = 2 * pair + sub
            k_h = jnp.where((lane // C_DK) == sub, k, 0.0)
            scores = _dot_nt(q, k_h) * dmat_ref[h]
            o_h = _dot(scores, v)
            o_pair = o_pair + jnp.where((lane // C_DV) == sub, o_h, 0.0)
            yield
        ks = k * kdec_ref[:, sl]
        st_ref[pair] = sdec_ref[pair] * st + jnp.where(r2 == c2, _dot(ks.T, v), 0.0)
        outs.append(o_pair)
        yield
    o = jnp.concatenate(outs, axis=1)
    gate = pc_ref[:, 2 * nqk + nv:2 * nqk + 2 * nv]
    o = o * lax.rsqrt(_head_mean_square(o, C_DV) + NORM_EPS) * gate
    o_ref[...] = o.astype(BF16)
    yield


def _retention_tables():
    n = MIX_BLOCK
    log_gamma = jnp.log1p(-jnp.exp2(-5.0 - jnp.arange(C_HEADS, dtype=F32)))
    idx = jnp.arange(n, dtype=F32)
    rel = idx[:, None] - idx[None, :]
    dmat = jnp.where(rel >= 0, jnp.exp(jnp.where(rel >= 0, log_gamma[:, None, None] * rel, 0.0)), 0.0)
    lane_gamma = jnp.repeat(log_gamma, C_DK)[None, :]
    qdec = jnp.exp(lane_gamma * (idx[:, None] + 1.0))
    kdec = jnp.exp(lane_gamma * (n - 1.0 - idx[:, None]))
    sdec = jnp.exp(lane_gamma * float(n)).reshape(C_HEADS // 2, LANES, 1)
    sdec = jnp.broadcast_to(sdec, (C_HEADS // 2, LANES, LANES))
    return dmat, qdec, kdec, sdec


def _mixers_kernel(pa_ref, al_ref, pb_ref, pg_ref, gate_ref, pc_ref, hnw_ref, gnw_ref, w_ref,
                   dmat_ref, qdec_ref, kdec_ref, sdec_ref, oa_ref, ob_ref, oc_ref,
                   hst_ref, hprev_ref, raw_ref, rst_ref, rprev_ref,
                   gst_ref, uw_ref, attn_ref, qe_ref, kt_ref, dec_ref, *, nblk):
    scratch = (hst_ref, hprev_ref, rst_ref, rprev_ref, gst_ref, uw_ref, attn_ref, qe_ref, kt_ref, dec_ref)

    @pl.when(pl.program_id(1) == 0)
    def _():
        for ref in scratch:
            ref[...] = jnp.zeros_like(ref)

    last = pl.program_id(1) == nblk
    others = itertools.chain(
        _hgrn_tasks(pa_ref, al_ref, hnw_ref, w_ref, oa_ref, hst_ref, hprev_ref, raw_ref, last),
        _ret_tasks(pc_ref, dmat_ref, qdec_ref, kdec_ref, sdec_ref, oc_ref, rst_ref, rprev_ref, last))

    def other_work():
        next(others, None)

    _gdn_body(pb_ref, pg_ref, gate_ref, gnw_ref, ob_ref, gst_ref, uw_ref, attn_ref, qe_ref, kt_ref, dec_ref,
              other_work)
    for _ in others:
        pass


def _mixers(pa, al, pb, pg, pc, layer, hgrn_nw, gdn_nw, tables, batch, seq):
    nblk = seq // MIX_BLOCK
    n = (MIX_BLOCK // CHUNK) * B_HEADS
    na, nz, nc = A_HEADS * A_DV, B_HEADS * B_DV, C_HEADS * C_DV
    consts = (_hgrn_decay_weights(),) + tuple(tables)
    lead = functools.partial(_lead_spec, nblk=nblk)
    state = lambda shape: pltpu.VMEM(shape, F32)
    return pl.pallas_call(
        functools.partial(_mixers_kernel, nblk=nblk),
        grid=(batch, nblk + 1),
        in_specs=[lead(A_WIDTH), lead(al.shape[1]), lead(B_WIDTH), lead(G_WIDTH),
                  _lag_spec(nz, nblk, (B_WIDTH - nz) // nz), lead(C_WIDTH),
                  _layer_spec(hgrn_nw.shape, layer), _layer_spec(gdn_nw.shape, layer)]
        + [_const_spec(c.shape) for c in consts],
        out_specs=[lead(na), _lag_spec(nz, nblk), lead(nc)],
        out_shape=[jax.ShapeDtypeStruct((batch * seq, width), BF16) for width in (na, nz, nc)],
        scratch_shapes=[
            state((A_HEADS // 2, 2 * A_DV, A_DK)), state((A_HEADS // 2, 2 * A_DV, A_DK)),
            state((MIX_BLOCK, na)),
            state((C_HEADS // 2, LANES, LANES)), state((C_HEADS // 2, LANES, LANES)),
            state((B_HEADS, B_DK, B_DV)),
            state((2, n, CHUNK, 2 * B_DV)), pltpu.VMEM((2, n, CHUNK, CHUNK), BF16),
            pltpu.VMEM((2, n, CHUNK, B_DK), BF16), pltpu.VMEM((2, n, B_DK, CHUNK), BF16),
            state((2, n, 1, B_DV))],
        compiler_params=_params(("arbitrary", "arbitrary")),
        name="mixers",
    )(pa, al, pb, pg, pb, pc, hgrn_nw, gdn_nw, *consts)


def _permute_w_in(w_in):
    na, nb = A_WIDTH, B_WIDTH
    gates = w_in[..., na + nb:na + nb + 2 * B_HEADS]
    rest = w_in[..., na + nb + 2 * B_HEADS:]
    pad = jnp.zeros(w_in.shape[:-1] + (G_WIDTH - 2 * B_HEADS,), w_in.dtype)
    return jnp.concatenate([w_in[..., :na + nb], rest, gates, pad], axis=-1)


def _lane_row(vals, offset):
    row = jnp.zeros((vals.shape[0], LANES), F32)
    return row.at[:, offset:offset + vals.shape[1]].set(vals.astype(F32))[:, None, :]


def kernel(x, positions, ffn1_norm, ffn1_w_gate, ffn1_w_up, ffn1_w_down, mix_norm, w_in,
           hgrn_lower_bounds, hgrn_norm, gdn_conv, gdn_a_log, gdn_dt_bias, gdn_norm, w_out,
           ffn2_norm, ffn2_w_gate, ffn2_w_up, ffn2_w_down, final_norm):
    batch, seq, d = x.shape
    depth = w_in.shape[0]
    t = batch * seq
    assert seq % MIX_BLOCK == 0 and seq % PROJ_BLOCK == 0 and t % ROW_BLOCK == 0
    assert t % (PROJ_BLOCK * LANES // (C_DK // 2)) == 0

    bf = lambda w: w.astype(BF16)
    row = lambda w: w.astype(F32)[:, None, :]
    w_in_p = bf(_permute_w_in(w_in))
    wg1, wu1, wd1 = bf(ffn1_w_gate), bf(ffn1_w_up), bf(ffn1_w_down)
    wg2, wu2, wd2 = bf(ffn2_w_gate), bf(ffn2_w_up), bf(ffn2_w_down)
    w_out_b = bf(w_out)
    n1, m1, n2 = row(ffn1_norm), row(mix_norm), row(ffn2_norm)
    hgrn_nw = row(jnp.tile(hgrn_norm, (1, A_HEADS)))
    gdn_nw = row(gdn_norm)
    alog = _lane_row(gdn_a_log, B_HEADS)
    dtb = _lane_row(gdn_dt_bias, B_HEADS)
    conv_w = gdn_conv.astype(F32)
    fw = final_norm.astype(F32)[None, :]

    lbs = _lower_bounds(hgrn_lower_bounds.astype(F32))[:, None, :]
    cos, sin = _rope_tables(positions)
    tables = _retention_tables()

    xt = x.reshape(t, d).astype(F32)
    for l in range(depth):
        xt = _ffn1(xt, l, n1, wg1, wu1, wd1)
        pa, al, pb, pg, pc = _proj(xt, l, m1, w_in_p, conv_w, lbs, alog, dtb, cos, sin, seq // PROJ_BLOCK)
        oa, ob, oc = _mixers(pa, al, pb, pg, pc, l, hgrn_nw, gdn_nw, tables, batch, seq)
        xt = _ffn2(xt, oa, ob, oc, l, w_out_b, n2, wg2, wu2, wd2, fw, l == depth - 1)
    return xt.reshape(batch, seq, d).astype(x.dtype)
```

```python
import functools
import itertools

import jax
import jax.numpy as jnp
import numpy as np
from jax import lax
from jax.experimental import pallas as pl
from jax.experimental.pallas import tpu as pltpu

F32 = jnp.float32
BF16 = jnp.bfloat16

NORM_EPS = 1e-6
LOG_FLOOR = 1e-20
ROPE_BASE = 10000.0

A_HEADS, A_DK, A_DV = 4, 128, 64
B_HEADS, B_DK, B_DV = 4, 128, 128
C_HEADS, C_DK, C_DV = 4, 64, 64
CONV_K = 4
LANES = 128
SUBLANES = 8

CHUNK = 64
MIX_BLOCK = 256
ROW_BLOCK = 1024
FFN_SUB_BLOCK = 256
PROJ_BLOCK = 256
PROJ_TILE = 256
VMEM_LIMIT = 56 * 1024 * 1024

A_WIDTH = 2 * A_HEADS * A_DK + 2 * A_HEADS * A_DV
B_WIDTH = 2 * B_HEADS * B_DK + 2 * B_HEADS * B_DV
C_WIDTH = 2 * C_HEADS * C_DK + 2 * C_HEADS * C_DV
G_WIDTH = LANES


def _dot(a, b):
    return jnp.dot(a.astype(BF16), b.astype(BF16), preferred_element_type=F32)


def _dot_nt(a, b):
    return lax.dot_general(a.astype(BF16), b.astype(BF16), (((1,), (1,)), ((), ())),
                           preferred_element_type=F32)


def _bdot(a, b):
    return jnp.dot(a, b, preferred_element_type=F32)


def _iota2(shape, dim):
    return lax.broadcasted_iota(jnp.int32, shape, dim)


def _softplus(x):
    return jnp.maximum(x, 0.0) + jnp.log1p(jnp.exp(-jnp.abs(x)))


def _split(a):
    hi = a.astype(BF16)
    lo = (a - hi.astype(F32)).astype(BF16)
    return hi, lo


def _split3(a):
    t1 = a.astype(BF16)
    rem = a - t1.astype(F32)
    t2 = rem.astype(BF16)
    return t1, t2, (rem - t2.astype(F32)).astype(BF16)


def _sigmoid(x):
    return jax.nn.sigmoid(x)


def _silu(x):
    return x * jax.nn.sigmoid(x)


def _rms(x, w):
    return x * lax.rsqrt(jnp.mean(x * x, axis=-1, keepdims=True) + NORM_EPS) * w


def _const_spec(shape):
    nd = len(shape)
    return pl.BlockSpec(shape, lambda *_: (0,) * nd, pipeline_mode=pl.Buffered(1))


def _layer_spec(shape, layer):
    return pl.BlockSpec((None,) + tuple(shape[1:]), lambda *_: (layer, 0, 0), pipeline_mode=pl.Buffered(1))


def _params(semantics):
    return pltpu.CompilerParams(dimension_semantics=semantics, vmem_limit_bytes=VMEM_LIMIT)


def _ffn_rows(x, nw_ref, wg_ref, wu_ref, wd_ref):
    h = _rms(x, nw_ref[...]).astype(BF16)
    g = jnp.dot(h, wg_ref[...], preferred_element_type=F32)
    u = jnp.dot(h, wu_ref[...], preferred_element_type=F32)
    a = (_silu(g) * u).astype(BF16)
    return x + 0.5 * jnp.dot(a, wd_ref[...], preferred_element_type=F32)


def _sub_blocks():
    return [slice(r, r + FFN_SUB_BLOCK) for r in range(0, ROW_BLOCK, FFN_SUB_BLOCK)]


def _ffn1_kernel(x_ref, nw_ref, wg_ref, wu_ref, wd_ref, o_ref):
    for rows in _sub_blocks():
        o_ref[rows, :] = _ffn_rows(x_ref[rows, :], nw_ref, wg_ref, wu_ref, wd_ref)


def _ffn2_kernel(x_ref, oa_ref, ob_ref, oc_ref, wo_ref, nw_ref, wg_ref, wu_ref, wd_ref, fw_ref, o_ref,
                 *, final):
    na, nb = oa_ref.shape[1], ob_ref.shape[1]
    for rows in _sub_blocks():
        x = x_ref[rows, :]
        x = x + jnp.dot(oa_ref[rows, :], wo_ref[0:na, :], preferred_element_type=F32)
        x = x + jnp.dot(ob_ref[rows, :], wo_ref[na:na + nb, :], preferred_element_type=F32)
        x = x + jnp.dot(oc_ref[rows, :], wo_ref[na + nb:, :], preferred_element_type=F32)
        y = _ffn_rows(x, nw_ref, wg_ref, wu_ref, wd_ref)
        if final:
            y = _rms(y, fw_ref[...])
        o_ref[rows, :] = y


def _ffn1(x, layer, nw, wg, wu, wd):
    t, d = x.shape
    row = pl.BlockSpec((ROW_BLOCK, d), lambda i: (i, 0))
    return pl.pallas_call(
        _ffn1_kernel,
        grid=(t // ROW_BLOCK,),
        in_specs=[row] + [_layer_spec(c.shape, layer) for c in (nw, wg, wu, wd)],
        out_specs=row,
        out_shape=jax.ShapeDtypeStruct((t, d), F32),
        compiler_params=_params(("parallel",)),
        name="ffn1",
    )(x, nw, wg, wu, wd)


def _ffn2(x, oa, ob, oc, layer, wo, nw, wg, wu, wd, fw, final):
    t, d = x.shape

    def rows(a):
        return pl.BlockSpec((ROW_BLOCK, a.shape[1]), lambda i: (i, 0))

    consts = (wo, nw, wg, wu, wd)
    return pl.pallas_call(
        functools.partial(_ffn2_kernel, final=final),
        grid=(t // ROW_BLOCK,),
        in_specs=([rows(x), rows(oa), rows(ob), rows(oc)] + [_layer_spec(c.shape, layer) for c in consts]
                  + [_const_spec(fw.shape)]),
        out_specs=rows(x),
        out_shape=jax.ShapeDtypeStruct((t, d), F32),
        compiler_params=_params(("parallel",)),
        name="ffn2",
    )(x, oa, ob, oc, *consts, fw)


def _proj_kernel(x_ref, nw_ref, w_ref, cw_ref, lb_ref, alog_ref, dtb_ref, cos_ref, sin_ref,
                 pa_ref, al_ref, pb_ref, pg_ref, pc_ref, tail_ref, *, nblk):
    @pl.when(pl.program_id(0) % nblk == 0)
    def _():
        tail_ref[...] = jnp.zeros_like(tail_ref)

    n = PROJ_BLOCK
    h = _rms(x_ref[...], nw_ref[...]).astype(BF16)
    off_b = A_WIDTH
    off_c = A_WIDTH + B_WIDTH
    off_g = A_WIDTH + B_WIDTH + C_WIDTH
    tile = PROJ_TILE

    def project(c0, width=tile):
        return _bdot(h, w_ref[:, c0:c0 + width])

    heavy, light = [], []

    def gates():
        pg = project(off_g, G_WIDTH)
        log_decay = -jnp.exp(alog_ref[...]) * _softplus(pg + dtb_ref[...])
        pg_ref[...] = jnp.where(_iota2((n, G_WIDTH), 1) < B_HEADS, _sigmoid(pg), log_decay)

    light.append(gates)

    nqk = B_HEADS * B_DK
    ncv = 2 * nqk + B_HEADS * B_DV
    row8 = _iota2((SUBLANES, tile), 0)

    def conv_tile(c0):
        x = project(off_b + c0)
        cw = cw_ref[:, c0:c0 + tile]
        prev = tail_ref[:, c0:c0 + tile]
        conv = x * cw[CONV_K - 1:CONV_K, :]
        for s in range(1, CONV_K):
            xs = pltpu.roll(x, s, 0)
            top = jnp.where(row8 < s, pltpu.roll(prev, s, 0), xs[0:SUBLANES, :])
            xs = jnp.concatenate([top, xs[SUBLANES:, :]], axis=0)
            conv = conv + xs * cw[CONV_K - 1 - s:CONV_K - s, :]
        tail_ref[:, c0:c0 + tile] = x[n - SUBLANES:, :]
        y = _silu(conv)
        if c0 >= 2 * nqk:
            pb_ref[:, c0:c0 + tile] = y
        else:
            scale = B_DK ** -0.5 if c0 < nqk else 1.0
            for j in range(0, tile, B_DK):
                yh = y[:, j:j + B_DK]
                inv_norm = lax.rsqrt(jnp.sum(yh * yh, axis=-1, keepdims=True) + NORM_EPS)
                pb_ref[:, c0 + j:c0 + j + B_DK] = yh * (inv_norm * scale)

    def plain_tile(out_ref, c_out, c_w, act):
        y = project(c_w)
        out_ref[:, c_out:c_out + tile] = _silu(y) if act else y

    heavy += [functools.partial(conv_tile, c0) for c0 in range(0, ncv, tile)]
    light += [functools.partial(plain_tile, pb_ref, c0, off_b + c0, True) for c0 in range(ncv, B_WIDTH, tile)]

    nk = A_HEADS * A_DK
    nv = A_HEADS * A_DV

    def forget_tile(c0):
        z = project(nk + c0)
        lb = lb_ref[:, c0:c0 + tile]
        f = lb + (1.0 - lb) * _sigmoid(z)
        pa_ref[:, nk + c0:nk + c0 + tile] = (1.0 - lb) * _sigmoid(-z)
        for i, term in enumerate(_split3(jnp.log2(jnp.maximum(f, LOG_FLOOR)))):
            al_ref[:, i * nk + c0:i * nk + c0 + tile] = term

    heavy += [functools.partial(forget_tile, c0) for c0 in range(0, nk, tile)]
    light += [functools.partial(plain_tile, pa_ref, c0, c0, True) for c0 in range(0, nk, tile)]
    light += [functools.partial(plain_tile, pa_ref, c0, c0, False) for c0 in range(2 * nk, 2 * nk + nv, tile)]
    light += [functools.partial(plain_tile, pa_ref, c0, c0, True) for c0 in range(2 * nk + nv, A_WIDTH, tile)]

    nqc = C_HEADS * C_DK
    nvc = C_HEADS * C_DV
    first_half = (_iota2((n, LANES), 1) % C_DK) < (C_DK // 2)

    def rope_tile(c0):
        x = project(off_c + c0)
        cos = cos_ref[...]
        sin = sin_ref[...]
        scale = 1.0 if c0 < nqc else C_DK ** -0.5
        for j in range(0, tile, LANES):
            xj = x[:, j:j + LANES]
            rot = jnp.where(first_half, pltpu.roll(xj, LANES - C_DK // 2, 1), pltpu.roll(xj, C_DK // 2, 1))
            pc_ref[:, c0 + j:c0 + j + LANES] = (xj * cos + rot * sin) * scale

    light += [functools.partial(rope_tile, c0) for c0 in range(0, 2 * nqc, tile)]
    light += [functools.partial(plain_tile, pc_ref, c0, off_c + c0, False)
              for c0 in range(2 * nqc, 2 * nqc + nvc, tile)]
    light += [functools.partial(plain_tile, pc_ref, c0, off_c + c0, True)
              for c0 in range(2 * nqc + nvc, C_WIDTH, tile)]

    while heavy or light:
        if light:
            light.pop(0)()
        if heavy:
            heavy.pop(0)()


def _proj(x, layer, nw, w, cw, lb, alog, dtb, cos, sin, nblk):
    t, d = x.shape
    consts = (nw, w, cw, lb, alog, dtb)
    outs = ((A_WIDTH, F32), (3 * A_HEADS * A_DK, BF16), (B_WIDTH, F32), (G_WIDTH, F32), (C_WIDTH, F32))

    def rows(width):
        return pl.BlockSpec((PROJ_BLOCK, width), lambda i: (i, 0))

    return pl.pallas_call(
        functools.partial(_proj_kernel, nblk=nblk),
        grid=(t // PROJ_BLOCK,),
        in_specs=[rows(d)] + [_layer_spec(c.shape, layer) for c in consts] + [rows(LANES), rows(LANES)],
        out_specs=[rows(n) for n, _ in outs],
        out_shape=[jax.ShapeDtypeStruct((t, n), dt) for n, dt in outs],
        scratch_shapes=[pltpu.VMEM((SUBLANES, B_HEADS * (2 * B_DK + B_DV)), F32)],
        compiler_params=_params(("arbitrary",)),
        name="in_proj",
    )(x, *consts, cos, sin)


def _lower_bounds_kernel(p_ref, o_ref):
    p = p_ref[...]
    e = jnp.exp(p - jnp.max(p, axis=0, keepdims=True))
    s = e / jnp.sum(e, axis=0, keepdims=True)
    rows = [s[0:1]]
    for l in range(1, p.shape[0]):
        rows.append(rows[-1] + s[l:l + 1])
    o_ref[...] = jnp.concatenate(rows, axis=0) - s[0:1]


def _lower_bounds(p):
    return pl.pallas_call(_lower_bounds_kernel, out_shape=jax.ShapeDtypeStruct(p.shape, F32),
                          name="hgrn_lower_bounds")(p)


def _rope_kernel(pos_ref, inv_ref, cos_ref, sin_ref):
    ang = pos_ref[...].astype(F32) * inv_ref[...]
    cos_ref[...] = jnp.cos(ang)
    sin_ref[...] = jnp.sin(ang)


def _rope_tables(positions):
    t = positions.size
    half = C_DK // 2
    per_row = LANES // half
    inv = ROPE_BASE ** (-jnp.arange(half, dtype=F32) / half)
    inv = jnp.tile(inv, per_row)[None, :]
    pos = jnp.repeat(positions.reshape(t // per_row, per_row), half, axis=1)
    rows = t // per_row
    blk = pl.BlockSpec((PROJ_BLOCK, LANES), lambda i: (i, 0))
    out = jax.ShapeDtypeStruct((rows, LANES), F32)
    cos, sin = pl.pallas_call(
        _rope_kernel,
        grid=(rows // PROJ_BLOCK,),
        in_specs=[blk, _const_spec(inv.shape)],
        out_specs=[blk, blk],
        out_shape=[out, out],
        compiler_params=_params(("parallel",)),
        name="rope_tables",
    )(pos, inv)
    cos = cos.reshape(t, half)
    sin = sin.reshape(t, half)
    reps = LANES // C_DK
    return jnp.tile(cos, (1, 2 * reps)), jnp.tile(jnp.concatenate([-sin, sin], axis=1), (1, reps))


def _head_mean_square(o, width):
    n = o.shape[1]
    r = _iota2((n, n), 0) // width
    c = _iota2((n, n), 1) // width
    ones = jnp.where(r == c, 1.0 / width, 0.0).astype(BF16)
    sq = o * o
    hi = sq.astype(BF16)
    lo = (sq - hi.astype(F32)).astype(BF16)
    return (jnp.dot(hi, ones, preferred_element_type=F32)
            + jnp.dot(lo, ones, preferred_element_type=F32))


def _lead_spec(width, nblk):
    return pl.BlockSpec((MIX_BLOCK, width), lambda b, s: (b * nblk + jnp.minimum(s, nblk - 1), 0))


def _lag_spec(width, nblk, col_block=0):
    return pl.BlockSpec((MIX_BLOCK, width), lambda b, s: (b * nblk + jnp.maximum(s - 1, 0), col_block))


HGRN_LEVELS = tuple(CHUNK >> (i + 1) for i in range(CHUNK.bit_length() - 1))


def _hgrn_decay_weights():
    n = CHUNK
    i = np.arange(n)[:, None]
    t = np.arange(n)[None, :]
    blocks = [t <= i, t > i]
    for h in HGRN_LEVELS:
        ref = (i // (2 * h)) * 2 * h + h - 1
        upper = (i % (2 * h)) >= h
        blocks.append(np.where(upper, (t > ref) & (t <= i), (t > i) & (t <= ref)))
    w = np.concatenate(blocks, axis=0).astype(np.float32)
    return jnp.asarray(np.concatenate([w, w, w], axis=1), dtype=BF16)


def _hgrn_tasks(pa_ref, al_ref, nw_ref, w_ref, o_ref, st_ref, raw_ref):
    nk = A_HEADS * A_DK
    nv = A_HEADS * A_DV
    heads = range(A_HEADS)
    nchunks = MIX_BLOCK // CHUNK
    r2 = _iota2((CHUNK, CHUNK), 0)
    c2 = _iota2((CHUNK, CHUNK), 1)
    row = _iota2((CHUNK, LANES), 0)
    lane = _iota2((CHUNK, LANES), 1)
    uppers = [(row % (2 * h)) >= h for h in HGRN_LEVELS]
    masks = [((r2 // (2 * h)) == (c2 // (2 * h))) & ((r2 % (2 * h)) >= h) & ((c2 % (2 * h)) < h)
             for h in HGRN_LEVELS]
    sts = [st_ref[pair] for pair in range(A_HEADS // 2)]

    for ci in range(nchunks):
        r0 = ci * CHUNK
        q = pa_ref[r0:r0 + CHUNK, 0:nk]
        k = pa_ref[r0:r0 + CHUNK, nk:2 * nk]
        log_f = jnp.concatenate([al_ref[r0:r0 + CHUNK, i * nk:(i + 1) * nk] for i in range(3)], axis=0)
        e = jnp.exp2(_bdot(w_ref[...], log_f))
        q_in = (q * e[0:CHUNK]).astype(BF16)
        k_st = (k * e[CHUNK:2 * CHUNK]).astype(BF16)
        decay = e[CHUNK - 1:CHUNK]
        yield
        attns = []
        for h in heads:
            sl = slice(h * A_DK, (h + 1) * A_DK)
            q_h, k_h = q[:, sl], k[:, sl]
            attn = jnp.where(r2 == c2, jnp.sum(q_h * k_h, axis=-1, keepdims=True), 0.0)
            for lv in range(len(HGRN_LEVELS)):
                e_lv = e[(2 + lv) * CHUNK:(3 + lv) * CHUNK, sl]
                x = (jnp.where(uppers[lv], q_h, k_h) * e_lv).astype(BF16)
                p = lax.dot_general(x, x, (((1,), (1,)), ((), ())), preferred_element_type=F32)
                attn = attn + jnp.where(masks[lv], p, 0.0)
            attns.append(attn.astype(BF16))
            yield
        outs = []
        for pair in range(A_HEADS // 2):
            vp = pa_ref[r0:r0 + CHUNK, 2 * nk + pair * LANES:2 * nk + (pair + 1) * LANES]
            vpb = vp.astype(BF16)
            vpt = vp.T.astype(BF16)
            st = sts[pair]
            stb = st.astype(BF16)
            o_pair = jnp.zeros((CHUNK, LANES), F32)
            new_rows = []
            for sub in range(2):
                h = 2 * pair + sub
                sl = slice(h * A_DK, (h + 1) * A_DK)
                o_h = (lax.dot_general(q_in[:, sl], stb, (((1,), (1,)), ((), ())),
                                       preferred_element_type=F32)
                       + _bdot(attns[h], vpb))
                o_pair = jnp.where((lane // A_DV) == sub, o_h, o_pair)
                rows = slice(sub * A_DV, (sub + 1) * A_DV)
                new_rows.append(decay[:, sl] * st[rows, :] + _bdot(vpt[rows, :], k_st[:, sl]))
            sts[pair] = jnp.concatenate(new_rows, axis=0)
            outs.append(o_pair)
        raw_ref[r0:r0 + CHUNK, :] = jnp.concatenate(outs, axis=1)
        yield
    for pair in range(A_HEADS // 2):
        st_ref[pair] = sts[pair]
    o = raw_ref[...]
    gate = pa_ref[:, 2 * nk + nv:2 * nk + 2 * nv]
    o = o * lax.rsqrt(_head_mean_square(o, A_DV) + NORM_EPS) * nw_ref[...] * gate
    o_ref[...] = o.astype(BF16)
    yield


def _packed_operands(a):
    n, w = a.shape
    hi = a.astype(BF16)
    hi_f = hi.astype(F32)
    lo_f = a - hi_f
    lo = lo_f.astype(BF16)
    col = jnp.where(_iota2((n, w), 1) < w // 2, hi_f, lo_f).astype(BF16)
    return jnp.concatenate([col, col], axis=1), jnp.concatenate([hi, hi, lo, lo], axis=0)


def _packed_rhs(b):
    hi, lo = _split(b)
    return jnp.concatenate([hi, hi, lo, lo], axis=0)


def _unit_lower_inverses(ms, between):
    n = ms[0].shape[0]
    r = _iota2((n, 2 * n), 0)
    c = _iota2((n, 2 * n), 1) % n
    eye = jnp.where(r == c, 1.0, 0.0).astype(F32)
    inv_ops = None
    inner = 1
    while inner < n:
        outer = inner * 4
        mask = ((r // outer) == (c // outer)) & ((r // inner) != (c // inner))
        ps = [jnp.where(mask, m, 0.0) for m in ms]
        if inv_ops is not None:
            ps = [_bdot(inv[0], _packed_rhs(p)) for inv, p in zip(inv_ops, ps)]
            between()
        p_ops = [_packed_operands(p) for p in ps]
        pps = [_bdot(po[0], po[1]) for po in p_ops]
        between()
        corrs = [_bdot(_packed_operands(eye - p)[0], _packed_rhs(eye + pp)) for p, pp in zip(ps, pps)]
        between()
        if inv_ops is not None:
            corrs = [_bdot(_packed_operands(cr)[0], inv[1]) for cr, inv in zip(corrs, inv_ops)]
            between()
        inv_ops = [_packed_operands(cr) for cr in corrs]
        inner = outer
    return [inv[0] for inv in inv_ops]


def _gdn_body(pb_ref, pg_ref, gate_ref, nw_ref, o_ref, st_ref, uw_ref, attn_ref, qe_ref, kt_ref, dec_ref,
              other_work, prepare):
    nqk = B_HEADS * B_DK
    heads = range(B_HEADS)
    nchunks = MIX_BLOCK // CHUNK
    nw = nw_ref[...]
    rd = pl.program_id(1) % 2
    wr = 1 - rd

    carry = {"st": [st_ref[h] for h in heads]}

    def first_half(ci):
        idx = [ci * B_HEADS + h for h in heads]
        st_bs = [st.astype(BF16) for st in carry["st"]]
        carry["vn"] = [(uw_ref[rd, i, :, 0:B_DV]
                        - _bdot(uw_ref[rd, i, :, B_DV:].astype(BF16), sb)).astype(BF16)
                       for i, sb in zip(idx, st_bs)]
        carry["o"] = [_bdot(qe_ref[rd, i], sb) for i, sb in zip(idx, st_bs)]

    def second_half(ci):
        r0 = ci * CHUNK
        idx = [ci * B_HEADS + h for h in heads]
        outs = [o + _bdot(attn_ref[rd, i], vn) for i, o, vn in zip(idx, carry["o"], carry["vn"])]
        carry["st"] = [dec_ref[rd, i] * st + _bdot(kt_ref[rd, i], vn)
                       for i, st, vn in zip(idx, carry["st"], carry["vn"])]
        for h in heads:
            o = outs[h]
            gate = gate_ref[r0:r0 + CHUNK, h * B_DV:(h + 1) * B_DV]
            o = o * lax.rsqrt(jnp.mean(o * o, axis=-1, keepdims=True) + NORM_EPS) * nw * gate
            o_ref[r0:r0 + CHUNK, h * B_DV:(h + 1) * B_DV] = o.astype(BF16)

    steps = iter([functools.partial(f, ci) for ci in range(nchunks) for f in (first_half, second_half)])

    def between():
        step = next(steps, None)
        if step is not None:
            step()
        other_work()
        other_work()

    if not prepare:
        for step in steps:
            step()
        for h in heads:
            st_ref[h] = carry["st"][h]
        return

    r2 = _iota2((CHUNK, CHUNK), 0)
    c2 = _iota2((CHUNK, CHUNK), 1)
    rdup = _iota2((CHUNK, 2 * CHUNK), 0)
    cdup = _iota2((CHUNK, 2 * CHUNK), 1) % CHUNK
    tril3 = jnp.where(_iota2((CHUNK, 3 * CHUNK), 0) >= _iota2((CHUNK, 3 * CHUNK), 1) % CHUNK,
                      1.0, 0.0).astype(BF16)
    ms, rhss = [], []
    for ci in range(nchunks):
        r0 = ci * CHUNK
        gates = pg_ref[r0:r0 + CHUNK, :]
        gcum = _bdot(tril3, jnp.concatenate(_split3(gates), axis=0))
        gcum_t = jnp.concatenate([gcum, gcum], axis=0).T

        for h in heads:
            q = pb_ref[r0:r0 + CHUNK, h * B_DK:(h + 1) * B_DK]
            k = pb_ref[r0:r0 + CHUNK, nqk + h * B_DK:nqk + (h + 1) * B_DK]
            v = pb_ref[r0:r0 + CHUNK, 2 * nqk + h * B_DV:2 * nqk + (h + 1) * B_DV]
            beta = jnp.broadcast_to(gates[:, h:h + 1], (CHUNK, LANES))
            gc = jnp.broadcast_to(gcum[:, B_HEADS + h:B_HEADS + h + 1], (CHUNK, LANES))
            gc_row = jnp.broadcast_to(gcum_t[B_HEADS + h:B_HEADS + h + 1, :], (CHUNK, 2 * CHUNK))
            gamma = jnp.exp(jnp.minimum(gc - gc_row, 0.0))
            kb = k.astype(BF16)
            kk = lax.dot_general(kb, jnp.concatenate([kb, kb], axis=0), (((1,), (1,)), ((), ())),
                                 preferred_element_type=F32)
            ms.append(jnp.where(rdup > cdup, beta * kk * gamma, 0.0))
            i = ci * B_HEADS + h
            attn_ref[wr, i] = jnp.where(r2 >= c2, _dot_nt(q, kb) * gamma[:, 0:CHUNK], 0.0).astype(BF16)
            e_gc = jnp.exp(gc)
            rhss.append(_packed_rhs(jnp.concatenate([v * beta, k * (beta * e_gc)], axis=1)))
            g_last = gc[CHUNK - 1:CHUNK, :]
            qe_ref[wr, i] = (q * e_gc).astype(BF16)
            kt_ref[wr, i] = (k * jnp.exp(g_last - gc)).T.astype(BF16)
            dec_ref[wr, i] = jnp.exp(g_last)
            other_work()

    t_ops = _unit_lower_inverses(ms, between)
    uws = [_bdot(t, rhs) for t, rhs in zip(t_ops, rhss)]
    other_work()
    for step in steps:
        step()
    for h in heads:
        st_ref[h] = carry["st"][h]
    for i in range(nchunks * B_HEADS):
        uw_ref[wr, i] = uws[i]


def _ret_tasks(pc_ref, dmat_ref, qdec_ref, kdec_ref, sdec_ref, o_ref, st_ref):
    nqk = C_HEADS * C_DK
    nv = C_HEADS * C_DV
    n = MIX_BLOCK
    lane = _iota2((n, LANES), 1)
    r2 = _iota2((LANES, LANES), 0) // C_DK
    c2 = _iota2((LANES, LANES), 1) // C_DV

    outs = []
    for pair in range(C_HEADS // 2):
        sl = slice(pair * LANES, (pair + 1) * LANES)
        q = pc_ref[:, sl]
        k = pc_ref[:, nqk + pair * LANES:nqk + (pair + 1) * LANES]
        v = pc_ref[:, 2 * nqk + pair * LANES:2 * nqk + (pair + 1) * LANES]
        st = st_ref[pair]
        o_pair = _dot(q * qdec_ref[:, sl], st)
        for sub in range(2):
            h = 2 * pair + sub
            k_h = jnp.where((lane // C_DK) == sub, k, 0.0)
            scores = _dot_nt(q, k_h) * dmat_ref[h]
            o_h = _dot(scores, v)
            o_pair = o_pair + jnp.where((lane // C_DV) == sub, o_h, 0.0)
            yield
        ks = k * kdec_ref[:, sl]
        st_ref[pair] = sdec_ref[pair] * st + jnp.where(r2 == c2, _dot(ks.T, v), 0.0)
        outs.append(o_pair)
        yield
    o = jnp.concatenate(outs, axis=1)
    gate = pc_ref[:, 2 * nqk + nv:2 * nqk + 2 * nv]
    o = o * lax.rsqrt(_head_mean_square(o, C_DV) + NORM_EPS) * gate
    o_ref[...] = o.astype(BF16)
    yield


def _retention_tables():
    n = MIX_BLOCK
    log_gamma = jnp.log1p(-jnp.exp2(-5.0 - jnp.arange(C_HEADS, dtype=F32)))
    idx = jnp.arange(n, dtype=F32)
    rel = idx[:, None] - idx[None, :]
    dmat = jnp.where(rel >= 0, jnp.exp(jnp.where(rel >= 0, log_gamma[:, None, None] * rel, 0.0)), 0.0)
    lane_gamma = jnp.repeat(log_gamma, C_DK)[None, :]
    qdec = jnp.exp(lane_gamma * (idx[:, None] + 1.0))
    kdec = jnp.exp(lane_gamma * (n - 1.0 - idx[:, None]))
    sdec = jnp.exp(lane_gamma * float(n)).reshape(C_HEADS // 2, LANES, 1)
    sdec = jnp.broadcast_to(sdec, (C_HEADS // 2, LANES, LANES))
    return dmat, qdec, kdec, sdec


def _mixers_kernel(pa_ref, al_ref, pb_ref, pg_ref, gate_ref, pc_ref, hnw_ref, gnw_ref, w_ref,
                   dmat_ref, qdec_ref, kdec_ref, sdec_ref, oa_ref, ob_ref, oc_ref,
                   hst_ref, raw_ref, rst_ref, gst_ref, uw_ref, attn_ref, qe_ref, kt_ref, dec_ref, *, nblk):
    step = pl.program_id(1)
    gdn_refs = (pb_ref, pg_ref, gate_ref, gnw_ref, ob_ref, gst_ref, uw_ref, attn_ref, qe_ref, kt_ref, dec_ref)

    @pl.when(step == 0)
    def _():
        for ref in (hst_ref, rst_ref, gst_ref, uw_ref, attn_ref, qe_ref, kt_ref, dec_ref):
            ref[...] = jnp.zeros_like(ref)

    @pl.when(step < nblk)
    def _():
        others = itertools.chain(
            _hgrn_tasks(pa_ref, al_ref, hnw_ref, w_ref, oa_ref, hst_ref, raw_ref),
            _ret_tasks(pc_ref, dmat_ref, qdec_ref, kdec_ref, sdec_ref, oc_ref, rst_ref))
        _gdn_body(*gdn_refs, lambda: next(others, None), True)
        for _ in others:
            pass

    @pl.when(step == nblk)
    def _():
        _gdn_body(*gdn_refs, lambda: None, False)


def _mixers(pa, al, pb, pg, pc, layer, hgrn_nw, gdn_nw, tables, batch, seq):
    nblk = seq // MIX_BLOCK
    n = (MIX_BLOCK // CHUNK) * B_HEADS
    na, nz, nc = A_HEADS * A_DV, B_HEADS * B_DV, C_HEADS * C_DV
    consts = (_hgrn_decay_weights(),) + tuple(tables)
    lead = functools.partial(_lead_spec, nblk=nblk)
    state = lambda shape: pltpu.VMEM(shape, F32)
    return pl.pallas_call(
        functools.partial(_mixers_kernel, nblk=nblk),
        grid=(batch, nblk + 1),
        in_specs=[lead(A_WIDTH), lead(al.shape[1]), lead(B_WIDTH), lead(G_WIDTH),
                  _lag_spec(nz, nblk, (B_WIDTH - nz) // nz), lead(C_WIDTH),
                  _layer_spec(hgrn_nw.shape, layer), _layer_spec(gdn_nw.shape, layer)]
        + [_const_spec(c.shape) for c in consts],
        out_specs=[lead(na), _lag_spec(nz, nblk), lead(nc)],
        out_shape=[jax.ShapeDtypeStruct((batch * seq, width), BF16) for width in (na, nz, nc)],
        scratch_shapes=[
            state((A_HEADS // 2, 2 * A_DV, A_DK)), state((MIX_BLOCK, na)),
            state((C_HEADS // 2, LANES, LANES)), state((B_HEADS, B_DK, B_DV)),
            state((2, n, CHUNK, 2 * B_DV)), pltpu.VMEM((2, n, CHUNK, CHUNK), BF16),
            pltpu.VMEM((2, n, CHUNK, B_DK), BF16), pltpu.VMEM((2, n, B_DK, CHUNK), BF16),
            state((2, n, 1, B_DV))],
        compiler_params=_params(("arbitrary", "arbitrary")),
        name="mixers",
    )(pa, al, pb, pg, pb, pc, hgrn_nw, gdn_nw, *consts)


def _permute_w_in(w_in):
    na, nb = A_WIDTH, B_WIDTH
    gates = w_in[..., na + nb:na + nb + 2 * B_HEADS]
    rest = w_in[..., na + nb + 2 * B_HEADS:]
    pad = jnp.zeros(w_in.shape[:-1] + (G_WIDTH - 2 * B_HEADS,), w_in.dtype)
    return jnp.concatenate([w_in[..., :na + nb], rest, gates, pad], axis=-1)


def _lane_row(vals, offset):
    row = jnp.zeros((vals.shape[0], LANES), F32)
    return row.at[:, offset:offset + vals.shape[1]].set(vals.astype(F32))[:, None, :]


def kernel(x, positions, ffn1_norm, ffn1_w_gate, ffn1_w_up, ffn1_w_down, mix_norm, w_in,
           hgrn_lower_bounds, hgrn_norm, gdn_conv, gdn_a_log, gdn_dt_bias, gdn_norm, w_out,
           ffn2_norm, ffn2_w_gate, ffn2_w_up, ffn2_w_down, final_norm):
    batch, seq, d = x.shape
    depth = w_in.shape[0]
    t = batch * seq
    assert seq % MIX_BLOCK == 0 and seq % PROJ_BLOCK == 0 and t % ROW_BLOCK == 0
    assert t % (PROJ_BLOCK * LANES // (C_DK // 2)) == 0

    bf = lambda w: w.astype(BF16)
    row = lambda w: w.astype(F32)[:, None, :]
    w_in_p = _permute_w_in(bf(w_in))
    wg1, wu1, wd1 = bf(ffn1_w_gate), bf(ffn1_w_up), bf(ffn1_w_down)
    wg2, wu2, wd2 = bf(ffn2_w_gate), bf(ffn2_w_up), bf(ffn2_w_down)
    w_out_b = bf(w_out)
    n1, m1, n2 = row(ffn1_norm), row(mix_norm), row(ffn2_norm)
    hgrn_nw = row(jnp.tile(hgrn_norm, (1, A_HEADS)))
    gdn_nw = row(gdn_norm)
    alog = _lane_row(gdn_a_log, B_HEADS)
    dtb = _lane_row(gdn_dt_bias, B_HEADS)
    conv_w = gdn_conv.astype(F32)
    fw = final_norm.astype(F32)[None, :]

    lbs = _lower_bounds(hgrn_lower_bounds.astype(F32))[:, None, :]
    cos, sin = _rope_tables(positions)
    tables = _retention_tables()

    xt = x.reshape(t, d).astype(F32)
    for l in range(depth):
        xt = _ffn1(xt, l, n1, wg1, wu1, wd1)
        pa, al, pb, pg, pc = _proj(xt, l, m1, w_in_p, conv_w, lbs, alog, dtb, cos, sin, seq // PROJ_BLOCK)
        oa, ob, oc = _mixers(pa, al, pb, pg, pc, l, hgrn_nw, gdn_nw, tables, batch, seq)
        xt = _ffn2(xt, oa, ob, oc, l, w_out_b, n2, wg2, wu2, wd2, fw, l == depth - 1)
    return xt.reshape(batch, seq, d).astype(x.dtype)
```

```python
import functools
import itertools

import jax
import jax.numpy as jnp
import numpy as np
from jax import lax
from jax.experimental import pallas as pl
from jax.experimental.pallas import tpu as pltpu

F32 = jnp.float32
BF16 = jnp.bfloat16

NORM_EPS = 1e-6
LOG_FLOOR = 1e-20
ROPE_BASE = 10000.0

A_HEADS, A_DK, A_DV = 4, 128, 64
B_HEADS, B_DK, B_DV = 4, 128, 128
C_HEADS, C_DK, C_DV = 4, 64, 64
CONV_K = 4
LANES = 128
SUBLANES = 8

CHUNK = 64
MIX_BLOCK = 256
ROW_BLOCK = 1024
FFN_SUB_BLOCK = 256
PROJ_BLOCK = 256
PROJ_TILE = 256
VMEM_LIMIT = 56 * 1024 * 1024

A_WIDTH = 2 * A_HEADS * A_DK + 2 * A_HEADS * A_DV
B_WIDTH = 2 * B_HEADS * B_DK + 2 * B_HEADS * B_DV
C_WIDTH = 2 * C_HEADS * C_DK + 2 * C_HEADS * C_DV
G_WIDTH = LANES


def _dot(a, b):
    return jnp.dot(a.astype(BF16), b.astype(BF16), preferred_element_type=F32)


def _dot_nt(a, b):
    return lax.dot_general(a.astype(BF16), b.astype(BF16), (((1,), (1,)), ((), ())),
                           preferred_element_type=F32)


def _bdot(a, b):
    return jnp.dot(a, b, preferred_element_type=F32)


def _iota2(shape, dim):
    return lax.broadcasted_iota(jnp.int32, shape, dim)


def _softplus(x):
    return jnp.maximum(x, 0.0) + jnp.log1p(jnp.exp(-jnp.abs(x)))


def _split(a):
    hi = a.astype(BF16)
    lo = (a - hi.astype(F32)).astype(BF16)
    return hi, lo


def _split3(a):
    t1 = a.astype(BF16)
    rem = a - t1.astype(F32)
    t2 = rem.astype(BF16)
    return t1, t2, (rem - t2.astype(F32)).astype(BF16)


def _sigmoid(x):
    return jax.nn.sigmoid(x)


def _silu(x):
    return x * jax.nn.sigmoid(x)


def _rms(x, w):
    return x * lax.rsqrt(jnp.mean(x * x, axis=-1, keepdims=True) + NORM_EPS) * w


def _const_spec(shape):
    nd = len(shape)
    return pl.BlockSpec(shape, lambda *_: (0,) * nd, pipeline_mode=pl.Buffered(1))


def _layer_spec(shape, layer):
    return pl.BlockSpec((None,) + tuple(shape[1:]), lambda *_: (layer, 0, 0), pipeline_mode=pl.Buffered(1))


def _params(semantics):
    return pltpu.CompilerParams(dimension_semantics=semantics, vmem_limit_bytes=VMEM_LIMIT)


def _ffn_rows(x, nw_ref, wg_ref, wu_ref, wd_ref):
    h = _rms(x, nw_ref[...]).astype(BF16)
    g = jnp.dot(h, wg_ref[...], preferred_element_type=F32)
    u = jnp.dot(h, wu_ref[...], preferred_element_type=F32)
    a = (_silu(g) * u).astype(BF16)
    return x + 0.5 * jnp.dot(a, wd_ref[...], preferred_element_type=F32)


def _sub_blocks():
    return [slice(r, r + FFN_SUB_BLOCK) for r in range(0, ROW_BLOCK, FFN_SUB_BLOCK)]


def _ffn1_kernel(x_ref, nw_ref, wg_ref, wu_ref, wd_ref, o_ref):
    for rows in _sub_blocks():
        o_ref[rows, :] = _ffn_rows(x_ref[rows, :], nw_ref, wg_ref, wu_ref, wd_ref)


def _ffn2_kernel(x_ref, oa_ref, ob_ref, oc_ref, wo_ref, nw_ref, wg_ref, wu_ref, wd_ref, fw_ref, o_ref,
                 *, final):
    na, nb = oa_ref.shape[1], ob_ref.shape[1]
    for rows in _sub_blocks():
        x = x_ref[rows, :]
        x = x + jnp.dot(oa_ref[rows, :], wo_ref[0:na, :], preferred_element_type=F32)
        x = x + jnp.dot(ob_ref[rows, :], wo_ref[na:na + nb, :], preferred_element_type=F32)
        x = x + jnp.dot(oc_ref[rows, :], wo_ref[na + nb:, :], preferred_element_type=F32)
        y = _ffn_rows(x, nw_ref, wg_ref, wu_ref, wd_ref)
        if final:
            y = _rms(y, fw_ref[...])
        o_ref[rows, :] = y


def _ffn1(x, layer, nw, wg, wu, wd):
    t, d = x.shape
    row = pl.BlockSpec((ROW_BLOCK, d), lambda i: (i, 0))
    return pl.pallas_call(
        _ffn1_kernel,
        grid=(t // ROW_BLOCK,),
        in_specs=[row] + [_layer_spec(c.shape, layer) for c in (nw, wg, wu, wd)],
        out_specs=row,
        out_shape=jax.ShapeDtypeStruct((t, d), F32),
        compiler_params=_params(("parallel",)),
        name="ffn1",
    )(x, nw, wg, wu, wd)


def _ffn2(x, oa, ob, oc, layer, wo, nw, wg, wu, wd, fw, final):
    t, d = x.shape

    def rows(a):
        return pl.BlockSpec((ROW_BLOCK, a.shape[1]), lambda i: (i, 0))

    consts = (wo, nw, wg, wu, wd)
    return pl.pallas_call(
        functools.partial(_ffn2_kernel, final=final),
        grid=(t // ROW_BLOCK,),
        in_specs=([rows(x), rows(oa), rows(ob), rows(oc)] + [_layer_spec(c.shape, layer) for c in consts]
                  + [_const_spec(fw.shape)]),
        out_specs=rows(x),
        out_shape=jax.ShapeDtypeStruct((t, d), F32),
        compiler_params=_params(("parallel",)),
        name="ffn2",
    )(x, oa, ob, oc, *consts, fw)


def _proj_kernel(x_ref, nw_ref, w_ref, cw_ref, lb_ref, alog_ref, dtb_ref, cos_ref, sin_ref,
                 pa_ref, al_ref, pb_ref, pg_ref, pc_ref, tail_ref, *, nblk):
    @pl.when(pl.program_id(0) % nblk == 0)
    def _():
        tail_ref[...] = jnp.zeros_like(tail_ref)

    n = PROJ_BLOCK
    h = _rms(x_ref[...], nw_ref[...]).astype(BF16)
    off_b = A_WIDTH
    off_c = A_WIDTH + B_WIDTH
    off_g = A_WIDTH + B_WIDTH + C_WIDTH
    tile = PROJ_TILE

    def project(c0, width=tile):
        return _bdot(h, w_ref[:, c0:c0 + width])

    heavy, light = [], []

    def gates():
        pg = project(off_g, G_WIDTH)
        log_decay = -jnp.exp(alog_ref[...]) * _softplus(pg + dtb_ref[...])
        pg_ref[...] = jnp.where(_iota2((n, G_WIDTH), 1) < B_HEADS, _sigmoid(pg), log_decay)

    light.append(gates)

    nqk = B_HEADS * B_DK
    ncv = 2 * nqk + B_HEADS * B_DV
    row8 = _iota2((SUBLANES, tile), 0)

    def conv_tile(c0):
        x = project(off_b + c0)
        cw = cw_ref[:, c0:c0 + tile]
        prev = tail_ref[:, c0:c0 + tile]
        conv = x * cw[CONV_K - 1:CONV_K, :]
        for s in range(1, CONV_K):
            xs = pltpu.roll(x, s, 0)
            top = jnp.where(row8 < s, pltpu.roll(prev, s, 0), xs[0:SUBLANES, :])
            xs = jnp.concatenate([top, xs[SUBLANES:, :]], axis=0)
            conv = conv + xs * cw[CONV_K - 1 - s:CONV_K - s, :]
        tail_ref[:, c0:c0 + tile] = x[n - SUBLANES:, :]
        y = _silu(conv)
        if c0 >= 2 * nqk:
            pb_ref[:, c0:c0 + tile] = y
        else:
            scale = B_DK ** -0.5 if c0 < nqk else 1.0
            for j in range(0, tile, B_DK):
                yh = y[:, j:j + B_DK]
                inv_norm = lax.rsqrt(jnp.sum(yh * yh, axis=-1, keepdims=True) + NORM_EPS)
                pb_ref[:, c0 + j:c0 + j + B_DK] = yh * (inv_norm * scale)

    def plain_tile(out_ref, c_out, c_w, act):
        y = project(c_w)
        out_ref[:, c_out:c_out + tile] = _silu(y) if act else y

    heavy += [functools.partial(conv_tile, c0) for c0 in range(0, ncv, tile)]
    light += [functools.partial(plain_tile, pb_ref, c0, off_b + c0, True) for c0 in range(ncv, B_WIDTH, tile)]

    nk = A_HEADS * A_DK
    nv = A_HEADS * A_DV

    def forget_tile(c0):
        z = project(nk + c0)
        lb = lb_ref[:, c0:c0 + tile]
        f = lb + (1.0 - lb) * _sigmoid(z)
        pa_ref[:, nk + c0:nk + c0 + tile] = (1.0 - lb) * _sigmoid(-z)
        for i, term in enumerate(_split3(jnp.log2(jnp.maximum(f, LOG_FLOOR)))):
            al_ref[:, i * nk + c0:i * nk + c0 + tile] = term

    heavy += [functools.partial(forget_tile, c0) for c0 in range(0, nk, tile)]
    light += [functools.partial(plain_tile, pa_ref, c0, c0, True) for c0 in range(0, nk, tile)]
    light += [functools.partial(plain_tile, pa_ref, c0, c0, False) for c0 in range(2 * nk, 2 * nk + nv, tile)]
    light += [functools.partial(plain_tile, pa_ref, c0, c0, True) for c0 in range(2 * nk + nv, A_WIDTH, tile)]

    nqc = C_HEADS * C_DK
    nvc = C_HEADS * C_DV
    first_half = (_iota2((n, LANES), 1) % C_DK) < (C_DK // 2)

    def rope_tile(c0):
        x = project(off_c + c0)
        cos = cos_ref[...]
        sin = sin_ref[...]
        scale = 1.0 if c0 < nqc else C_DK ** -0.5
        for j in range(0, tile, LANES):
            xj = x[:, j:j + LANES]
            rot = jnp.where(first_half, pltpu.roll(xj, LANES - C_DK // 2, 1), pltpu.roll(xj, C_DK // 2, 1))
            pc_ref[:, c0 + j:c0 + j + LANES] = (xj * cos + rot * sin) * scale

    light += [functools.partial(rope_tile, c0) for c0 in range(0, 2 * nqc, tile)]
    light += [functools.partial(plain_tile, pc_ref, c0, off_c + c0, False)
              for c0 in range(2 * nqc, 2 * nqc + nvc, tile)]
    light += [functools.partial(plain_tile, pc_ref, c0, off_c + c0, True)
              for c0 in range(2 * nqc + nvc, C_WIDTH, tile)]

    while heavy or light:
        if heavy:
            heavy.pop(0)()
        if light:
            light.pop(0)()


def _proj(x, layer, nw, w, cw, lb, alog, dtb, cos, sin, nblk):
    t, d = x.shape
    consts = (nw, w, cw, lb, alog, dtb)
    outs = ((A_WIDTH, F32), (3 * A_HEADS * A_DK, BF16), (B_WIDTH, F32), (G_WIDTH, F32), (C_WIDTH, F32))

    def rows(width):
        return pl.BlockSpec((PROJ_BLOCK, width), lambda i: (i, 0))

    return pl.pallas_call(
        functools.partial(_proj_kernel, nblk=nblk),
        grid=(t // PROJ_BLOCK,),
        in_specs=[rows(d)] + [_layer_spec(c.shape, layer) for c in consts] + [rows(LANES), rows(LANES)],
        out_specs=[rows(n) for n, _ in outs],
        out_shape=[jax.ShapeDtypeStruct((t, n), dt) for n, dt in outs],
        scratch_shapes=[pltpu.VMEM((SUBLANES, B_HEADS * (2 * B_DK + B_DV)), F32)],
        compiler_params=_params(("arbitrary",)),
        name="in_proj",
    )(x, *consts, cos, sin)


def _lower_bounds_kernel(p_ref, o_ref):
    p = p_ref[...]
    e = jnp.exp(p - jnp.max(p, axis=0, keepdims=True))
    s = e / jnp.sum(e, axis=0, keepdims=True)
    rows = [s[0:1]]
    for l in range(1, p.shape[0]):
        rows.append(rows[-1] + s[l:l + 1])
    o_ref[...] = jnp.concatenate(rows, axis=0) - s[0:1]


def _lower_bounds(p):
    return pl.pallas_call(_lower_bounds_kernel, out_shape=jax.ShapeDtypeStruct(p.shape, F32),
                          name="hgrn_lower_bounds")(p)


def _rope_kernel(pos_ref, inv_ref, cos_ref, sin_ref):
    ang = pos_ref[...].astype(F32) * inv_ref[...]
    cos_ref[...] = jnp.cos(ang)
    sin_ref[...] = jnp.sin(ang)


def _rope_tables(positions):
    t = positions.size
    half = C_DK // 2
    per_row = LANES // half
    inv = ROPE_BASE ** (-jnp.arange(half, dtype=F32) / half)
    inv = jnp.tile(inv, per_row)[None, :]
    pos = jnp.repeat(positions.reshape(t // per_row, per_row), half, axis=1)
    rows = t // per_row
    blk = pl.BlockSpec((PROJ_BLOCK, LANES), lambda i: (i, 0))
    out = jax.ShapeDtypeStruct((rows, LANES), F32)
    cos, sin = pl.pallas_call(
        _rope_kernel,
        grid=(rows // PROJ_BLOCK,),
        in_specs=[blk, _const_spec(inv.shape)],
        out_specs=[blk, blk],
        out_shape=[out, out],
        compiler_params=_params(("parallel",)),
        name="rope_tables",
    )(pos, inv)
    cos = cos.reshape(t, half)
    sin = sin.reshape(t, half)
    reps = LANES // C_DK
    return jnp.tile(cos, (1, 2 * reps)), jnp.tile(jnp.concatenate([-sin, sin], axis=1), (1, reps))


def _head_mean_square(o, width):
    n = o.shape[1]
    r = _iota2((n, n), 0) // width
    c = _iota2((n, n), 1) // width
    ones = jnp.where(r == c, 1.0 / width, 0.0).astype(BF16)
    sq = o * o
    hi = sq.astype(BF16)
    lo = (sq - hi.astype(F32)).astype(BF16)
    return (jnp.dot(hi, ones, preferred_element_type=F32)
            + jnp.dot(lo, ones, preferred_element_type=F32))


def _lead_spec(width, nblk):
    return pl.BlockSpec((MIX_BLOCK, width), lambda b, s: (b * nblk + jnp.minimum(s, nblk - 1), 0))


def _lag_spec(width, nblk, col_block=0):
    return pl.BlockSpec((MIX_BLOCK, width), lambda b, s: (b * nblk + jnp.maximum(s - 1, 0), col_block))


HGRN_LEVELS = tuple(CHUNK >> (i + 1) for i in range(CHUNK.bit_length() - 1))


def _hgrn_decay_weights():
    n = CHUNK
    i = np.arange(n)[:, None]
    t = np.arange(n)[None, :]
    blocks = [t <= i, t > i]
    for h in HGRN_LEVELS:
        ref = (i // (2 * h)) * 2 * h + h - 1
        upper = (i % (2 * h)) >= h
        blocks.append(np.where(upper, (t > ref) & (t <= i), (t > i) & (t <= ref)))
    w = np.concatenate(blocks, axis=0).astype(np.float32)
    return jnp.asarray(np.concatenate([w, w, w], axis=1), dtype=BF16)


def _hgrn_tasks(pa_ref, al_ref, nw_ref, w_ref, o_ref, st_ref, raw_ref):
    nk = A_HEADS * A_DK
    nv = A_HEADS * A_DV
    heads = range(A_HEADS)
    nchunks = MIX_BLOCK // CHUNK
    r2 = _iota2((CHUNK, CHUNK), 0)
    c2 = _iota2((CHUNK, CHUNK), 1)
    row = _iota2((CHUNK, LANES), 0)
    lane = _iota2((CHUNK, LANES), 1)
    uppers = [(row % (2 * h)) >= h for h in HGRN_LEVELS]
    masks = [((r2 // (2 * h)) == (c2 // (2 * h))) & ((r2 % (2 * h)) >= h) & ((c2 % (2 * h)) < h)
             for h in HGRN_LEVELS]
    sts = [st_ref[pair] for pair in range(A_HEADS // 2)]

    for ci in range(nchunks):
        r0 = ci * CHUNK
        q = pa_ref[r0:r0 + CHUNK, 0:nk]
        k = pa_ref[r0:r0 + CHUNK, nk:2 * nk]
        log_f = jnp.concatenate([al_ref[r0:r0 + CHUNK, i * nk:(i + 1) * nk] for i in range(3)], axis=0)
        e = jnp.exp2(_bdot(w_ref[...], log_f))
        q_in = (q * e[0:CHUNK]).astype(BF16)
        k_st = (k * e[CHUNK:2 * CHUNK]).astype(BF16)
        decay = e[CHUNK - 1:CHUNK]
        yield
        attns = []
        for h in heads:
            sl = slice(h * A_DK, (h + 1) * A_DK)
            q_h, k_h = q[:, sl], k[:, sl]
            attn = jnp.where(r2 == c2, jnp.sum(q_h * k_h, axis=-1, keepdims=True), 0.0)
            for lv in range(len(HGRN_LEVELS)):
                e_lv = e[(2 + lv) * CHUNK:(3 + lv) * CHUNK, sl]
                x = (jnp.where(uppers[lv], q_h, k_h) * e_lv).astype(BF16)
                p = lax.dot_general(x, x, (((1,), (1,)), ((), ())), preferred_element_type=F32)
                attn = attn + jnp.where(masks[lv], p, 0.0)
            attns.append(attn.astype(BF16))
            yield
        outs = []
        for pair in range(A_HEADS // 2):
            vp = pa_ref[r0:r0 + CHUNK, 2 * nk + pair * LANES:2 * nk + (pair + 1) * LANES]
            vpb = vp.astype(BF16)
            vpt = vp.T.astype(BF16)
            st = sts[pair]
            stb = st.astype(BF16)
            o_pair = jnp.zeros((CHUNK, LANES), F32)
            new_rows = []
            for sub in range(2):
                h = 2 * pair + sub
                sl = slice(h * A_DK, (h + 1) * A_DK)
                o_h = (lax.dot_general(q_in[:, sl], stb, (((1,), (1,)), ((), ())),
                                       preferred_element_type=F32)
                       + _bdot(attns[h], vpb))
                o_pair = jnp.where((lane // A_DV) == sub, o_h, o_pair)
                rows = slice(sub * A_DV, (sub + 1) * A_DV)
                new_rows.append(decay[:, sl] * st[rows, :] + _bdot(vpt[rows, :], k_st[:, sl]))
            sts[pair] = jnp.concatenate(new_rows, axis=0)
            outs.append(o_pair)
        raw_ref[r0:r0 + CHUNK, :] = jnp.concatenate(outs, axis=1)
        yield
    for pair in range(A_HEADS // 2):
        st_ref[pair] = sts[pair]
    o = raw_ref[...]
    gate = pa_ref[:, 2 * nk + nv:2 * nk + 2 * nv]
    o = o * lax.rsqrt(_head_mean_square(o, A_DV) + NORM_EPS) * nw_ref[...] * gate
    o_ref[...] = o.astype(BF16)
    yield


def _packed_operands(a):
    n, w = a.shape
    hi = a.astype(BF16)
    hi_f = hi.astype(F32)
    lo_f = a - hi_f
    lo = lo_f.astype(BF16)
    col = jnp.where(_iota2((n, w), 1) < w // 2, hi_f, lo_f).astype(BF16)
    return jnp.concatenate([col, col], axis=1), jnp.concatenate([hi, hi, lo, lo], axis=0)


def _packed_rhs(b):
    hi, lo = _split(b)
    return jnp.concatenate([hi, hi, lo, lo], axis=0)


def _unit_lower_inverses(ms, between):
    n = ms[0].shape[0]
    r = _iota2((n, 2 * n), 0)
    c = _iota2((n, 2 * n), 1) % n
    eye = jnp.where(r == c, 1.0, 0.0).astype(F32)
    inv_ops = None
    inner = 1
    while inner < n:
        outer = inner * 4
        mask = ((r // outer) == (c // outer)) & ((r // inner) != (c // inner))
        ps = [jnp.where(mask, m, 0.0) for m in ms]
        if inv_ops is not None:
            ps = [_bdot(inv[0], _packed_rhs(p)) for inv, p in zip(inv_ops, ps)]
            between()
        p_ops = [_packed_operands(p) for p in ps]
        pps = [_bdot(po[0], po[1]) for po in p_ops]
        between()
        corrs = [_bdot(_packed_operands(eye - p)[0], _packed_rhs(eye + pp)) for p, pp in zip(ps, pps)]
        between()
        if inv_ops is not None:
            corrs = [_bdot(_packed_operands(cr)[0], inv[1]) for cr, inv in zip(corrs, inv_ops)]
            between()
        inv_ops = [_packed_operands(cr) for cr in corrs]
        inner = outer
    return [inv[0] for inv in inv_ops]


def _gdn_body(pb_ref, pg_ref, gate_ref, nw_ref, o_ref, st_ref, uw_ref, attn_ref, qe_ref, kt_ref, dec_ref,
              other_work, prepare):
    nqk = B_HEADS * B_DK
    heads = range(B_HEADS)
    nchunks = MIX_BLOCK // CHUNK
    nw = nw_ref[...]
    rd = pl.program_id(1) % 2
    wr = 1 - rd

    carry = {"st": [st_ref[h] for h in heads]}

    def first_half(ci):
        idx = [ci * B_HEADS + h for h in heads]
        st_bs = [st.astype(BF16) for st in carry["st"]]
        carry["vn"] = [(uw_ref[rd, i, :, 0:B_DV]
                        - _bdot(uw_ref[rd, i, :, B_DV:].astype(BF16), sb)).astype(BF16)
                       for i, sb in zip(idx, st_bs)]
        carry["o"] = [_bdot(qe_ref[rd, i], sb) for i, sb in zip(idx, st_bs)]

    def second_half(ci):
        r0 = ci * CHUNK
        idx = [ci * B_HEADS + h for h in heads]
        outs = [o + _bdot(attn_ref[rd, i], vn) for i, o, vn in zip(idx, carry["o"], carry["vn"])]
        carry["st"] = [dec_ref[rd, i] * st + _bdot(kt_ref[rd, i], vn)
                       for i, st, vn in zip(idx, carry["st"], carry["vn"])]
        for h in heads:
            o = outs[h]
            gate = gate_ref[r0:r0 + CHUNK, h * B_DV:(h + 1) * B_DV]
            o = o * lax.rsqrt(jnp.mean(o * o, axis=-1, keepdims=True) + NORM_EPS) * nw * gate
            o_ref[r0:r0 + CHUNK, h * B_DV:(h + 1) * B_DV] = o.astype(BF16)

    steps = iter([functools.partial(f, ci) for ci in range(nchunks) for f in (first_half, second_half)])

    def between():
        step = next(steps, None)
        if step is not None:
            step()
        other_work()
        other_work()

    if not prepare:
        for step in steps:
            step()
        for h in heads:
            st_ref[h] = carry["st"][h]
        return

    r2 = _iota2((CHUNK, CHUNK), 0)
    c2 = _iota2((CHUNK, CHUNK), 1)
    rdup = _iota2((CHUNK, 2 * CHUNK), 0)
    cdup = _iota2((CHUNK, 2 * CHUNK), 1) % CHUNK
    tril3 = jnp.where(_iota2((CHUNK, 3 * CHUNK), 0) >= _iota2((CHUNK, 3 * CHUNK), 1) % CHUNK,
                      1.0, 0.0).astype(BF16)
    ms, rhss = [], []
    for ci in range(nchunks):
        r0 = ci * CHUNK
        gates = pg_ref[r0:r0 + CHUNK, :]
        gcum = _bdot(tril3, jnp.concatenate(_split3(gates), axis=0))
        gcum_t = jnp.concatenate([gcum, gcum], axis=0).T

        for h in heads:
            q = pb_ref[r0:r0 + CHUNK, h * B_DK:(h + 1) * B_DK]
            k = pb_ref[r0:r0 + CHUNK, nqk + h * B_DK:nqk + (h + 1) * B_DK]
            v = pb_ref[r0:r0 + CHUNK, 2 * nqk + h * B_DV:2 * nqk + (h + 1) * B_DV]
            beta = jnp.broadcast_to(gates[:, h:h + 1], (CHUNK, LANES))
            gc = jnp.broadcast_to(gcum[:, B_HEADS + h:B_HEADS + h + 1], (CHUNK, LANES))
            gc_row = jnp.broadcast_to(gcum_t[B_HEADS + h:B_HEADS + h + 1, :], (CHUNK, 2 * CHUNK))
            gamma = jnp.exp(jnp.minimum(gc - gc_row, 0.0))
            kb = k.astype(BF16)
            kk = lax.dot_general(kb, jnp.concatenate([kb, kb], axis=0), (((1,), (1,)), ((), ())),
                                 preferred_element_type=F32)
            ms.append(jnp.where(rdup > cdup, beta * kk * gamma, 0.0))
            i = ci * B_HEADS + h
            attn_ref[wr, i] = jnp.where(r2 >= c2, _dot_nt(q, kb) * gamma[:, 0:CHUNK], 0.0).astype(BF16)
            e_gc = jnp.exp(gc)
            rhss.append(_packed_rhs(jnp.concatenate([v * beta, k * (beta * e_gc)], axis=1)))
            g_last = gc[CHUNK - 1:CHUNK, :]
            qe_ref[wr, i] = (q * e_gc).astype(BF16)
            kt_ref[wr, i] = (k * jnp.exp(g_last - gc)).T.astype(BF16)
            dec_ref[wr, i] = jnp.exp(g_last)
            other_work()

    t_ops = _unit_lower_inverses(ms, between)
    uws = [_bdot(t, rhs) for t, rhs in zip(t_ops, rhss)]
    other_work()
    for step in steps:
        step()
    for h in heads:
        st_ref[h] = carry["st"][h]
    for i in range(nchunks * B_HEADS):
        uw_ref[wr, i] = uws[i]


def _ret_tasks(pc_ref, dmat_ref, qdec_ref, kdec_ref, sdec_ref, o_ref, st_ref):
    nqk = C_HEADS * C_DK
    nv = C_HEADS * C_DV
    n = MIX_BLOCK
    lane = _iota2((n, LANES), 1)
    r2 = _iota2((LANES, LANES), 0) // C_DK
    c2 = _iota2((LANES, LANES), 1) // C_DV

    outs = []
    for pair in range(C_HEADS // 2):
        sl = slice(pair * LANES, (pair + 1) * LANES)
        q = pc_ref[:, sl]
        k = pc_ref[:, nqk + pair * LANES:nqk + (pair + 1) * LANES]
        v = pc_ref[:, 2 * nqk + pair * LANES:2 * nqk + (pair + 1) * LANES]
        st = st_ref[pair]
        o_pair = _dot(q * qdec_ref[:, sl], st)
        for sub in range(2):
            h = 2 * pair + sub
            k_h = jnp.where((lane // C_DK) == sub, k, 0.0)
            scores = _dot_nt(q, k_h) * dmat_ref[h]
            o_h = _dot(scores, v)
            o_pair = o_pair + jnp.where((lane // C_DV) == sub, o_h, 0.0)
            yield
        ks = k * kdec_ref[:, sl]
        st_ref[pair] = sdec_ref[pair] * st + jnp.where(r2 == c2, _dot(ks.T, v), 0.0)
        outs.append(o_pair)
        yield
    o = jnp.concatenate(outs, axis=1)
    gate = pc_ref[:, 2 * nqk + nv:2 * nqk + 2 * nv]
    o = o * lax.rsqrt(_head_mean_square(o, C_DV) + NORM_EPS) * gate
    o_ref[...] = o.astype(BF16)
    yield


def _retention_tables():
    n = MIX_BLOCK
    log_gamma = jnp.log1p(-jnp.exp2(-5.0 - jnp.arange(C_HEADS, dtype=F32)))
    idx = jnp.arange(n, dtype=F32)
    rel = idx[:, None] - idx[None, :]
    dmat = jnp.where(rel >= 0, jnp.exp(jnp.where(rel >= 0, log_gamma[:, None, None] * rel, 0.0)), 0.0)
    lane_gamma = jnp.repeat(log_gamma, C_DK)[None, :]
    qdec = jnp.exp(lane_gamma * (idx[:, None] + 1.0))
    kdec = jnp.exp(lane_gamma * (n - 1.0 - idx[:, None]))
    sdec = jnp.exp(lane_gamma * float(n)).reshape(C_HEADS // 2, LANES, 1)
    sdec = jnp.broadcast_to(sdec, (C_HEADS // 2, LANES, LANES))
    return dmat, qdec, kdec, sdec


def _mixers_kernel(pa_ref, al_ref, pb_ref, pg_ref, gate_ref, pc_ref, hnw_ref, gnw_ref, w_ref,
                   dmat_ref, qdec_ref, kdec_ref, sdec_ref, oa_ref, ob_ref, oc_ref,
                   hst_ref, raw_ref, rst_ref, gst_ref, uw_ref, attn_ref, qe_ref, kt_ref, dec_ref, *, nblk):
    step = pl.program_id(1)
    gdn_refs = (pb_ref, pg_ref, gate_ref, gnw_ref, ob_ref, gst_ref, uw_ref, attn_ref, qe_ref, kt_ref, dec_ref)

    @pl.when(step == 0)
    def _():
        for ref in (hst_ref, rst_ref, gst_ref, uw_ref, attn_ref, qe_ref, kt_ref, dec_ref):
            ref[...] = jnp.zeros_like(ref)

    @pl.when(step < nblk)
    def _():
        others = itertools.chain(
            _hgrn_tasks(pa_ref, al_ref, hnw_ref, w_ref, oa_ref, hst_ref, raw_ref),
            _ret_tasks(pc_ref, dmat_ref, qdec_ref, kdec_ref, sdec_ref, oc_ref, rst_ref))
        _gdn_body(*gdn_refs, lambda: next(others, None), True)
        for _ in others:
            pass

    @pl.when(step == nblk)
    def _():
        _gdn_body(*gdn_refs, lambda: None, False)


def _mixers(pa, al, pb, pg, pc, layer, hgrn_nw, gdn_nw, tables, batch, seq):
    nblk = seq // MIX_BLOCK
    n = (MIX_BLOCK // CHUNK) * B_HEADS
    na, nz, nc = A_HEADS * A_DV, B_HEADS * B_DV, C_HEADS * C_DV
    consts = (_hgrn_decay_weights(),) + tuple(tables)
    lead = functools.partial(_lead_spec, nblk=nblk)
    state = lambda shape: pltpu.VMEM(shape, F32)
    return pl.pallas_call(
        functools.partial(_mixers_kernel, nblk=nblk),
        grid=(batch, nblk + 1),
        in_specs=[lead(A_WIDTH), lead(al.shape[1]), lead(B_WIDTH), lead(G_WIDTH),
                  _lag_spec(nz, nblk, (B_WIDTH - nz) // nz), lead(C_WIDTH),
                  _layer_spec(hgrn_nw.shape, layer), _layer_spec(gdn_nw.shape, layer)]
        + [_const_spec(c.shape) for c in consts],
        out_specs=[lead(na), _lag_spec(nz, nblk), lead(nc)],
        out_shape=[jax.ShapeDtypeStruct((batch * seq, width), BF16) for width in (na, nz, nc)],
        scratch_shapes=[
            state((A_HEADS // 2, 2 * A_DV, A_DK)), state((MIX_BLOCK, na)),
            state((C_HEADS // 2, LANES, LANES)), state((B_HEADS, B_DK, B_DV)),
            state((2, n, CHUNK, 2 * B_DV)), pltpu.VMEM((2, n, CHUNK, CHUNK), BF16),
            pltpu.VMEM((2, n, CHUNK, B_DK), BF16), pltpu.VMEM((2, n, B_DK, CHUNK), BF16),
            state((2, n, 1, B_DV))],
        compiler_params=_params(("arbitrary", "arbitrary")),
        name="mixers",
    )(pa, al, pb, pg, pb, pc, hgrn_nw, gdn_nw, *consts)


def _permute_w_in(w_in):
    na, nb = A_WIDTH, B_WIDTH
    gates = w_in[..., na + nb:na + nb + 2 * B_HEADS]
    rest = w_in[..., na + nb + 2 * B_HEADS:]
    pad = jnp.zeros(w_in.shape[:-1] + (G_WIDTH - 2 * B_HEADS,), w_in.dtype)
    return jnp.concatenate([w_in[..., :na + nb], rest, gates, pad], axis=-1)


def _lane_row(vals, offset):
    row = jnp.zeros((vals.shape[0], LANES), F32)
    return row.at[:, offset:offset + vals.shape[1]].set(vals.astype(F32))[:, None, :]


def kernel(x, positions, ffn1_norm, ffn1_w_gate, ffn1_w_up, ffn1_w_down, mix_norm, w_in,
           hgrn_lower_bounds, hgrn_norm, gdn_conv, gdn_a_log, gdn_dt_bias, gdn_norm, w_out,
           ffn2_norm, ffn2_w_gate, ffn2_w_up, ffn2_w_down, final_norm):
    batch, seq, d = x.shape
    depth = w_in.shape[0]
    t = batch * seq
    assert seq % MIX_BLOCK == 0 and seq % PROJ_BLOCK == 0 and t % ROW_BLOCK == 0
    assert t % (PROJ_BLOCK * LANES // (C_DK // 2)) == 0

    bf = lambda w: w.astype(BF16)
    row = lambda w: w.astype(F32)[:, None, :]
    w_in_p = _permute_w_in(bf(w_in))
    wg1, wu1, wd1 = bf(ffn1_w_gate), bf(ffn1_w_up), bf(ffn1_w_down)
    wg2, wu2, wd2 = bf(ffn2_w_gate), bf(ffn2_w_up), bf(ffn2_w_down)
    w_out_b = bf(w_out)
    n1, m1, n2 = row(ffn1_norm), row(mix_norm), row(ffn2_norm)
    hgrn_nw = row(jnp.tile(hgrn_norm, (1, A_HEADS)))
    gdn_nw = row(gdn_norm)
    alog = _lane_row(gdn_a_log, B_HEADS)
    dtb = _lane_row(gdn_dt_bias, B_HEADS)
    conv_w = gdn_conv.astype(F32)
    fw = final_norm.astype(F32)[None, :]

    lbs = _lower_bounds(hgrn_lower_bounds.astype(F32))[:, None, :]
    cos, sin = _rope_tables(positions)
    tables = _retention_tables()

    xt = x.reshape(t, d).astype(F32)
    for l in range(depth):
        xt = _ffn1(xt, l, n1, wg1, wu1, wd1)
        pa, al, pb, pg, pc = _proj(xt, l, m1, w_in_p, conv_w, lbs, alog, dtb, cos, sin, seq // PROJ_BLOCK)
        oa, ob, oc = _mixers(pa, al, pb, pg, pc, l, hgrn_nw, gdn_nw, tables, batch, seq)
        xt = _ffn2(xt, oa, ob, oc, l, w_out_b, n2, wg2, wu2, wd2, fw, l == depth - 1)
    return xt.reshape(batch, seq, d).astype(x.dtype)
```
